```python
import jax, jax.numpy as jnp
from jax import lax
import numpy as np

D_MODEL = 1024
BATCH = 32
SEQ = 2048
DEPTH = 1

N_META = 16
MIX_WIDTH = D_MODEL
CONV_DIM = MIX_WIDTH // 2
ATTN_DIM = MIX_WIDTH - CONV_DIM
HEAD_DIM = 64
N_HEADS = ATTN_DIM // HEAD_DIM
N_CONV_GROUPS = CONV_DIM // HEAD_DIM
CONV_K = 3
D_FF = ((8 * D_MODEL // 3 + 255) // 256) * 256
Q_BLOCK = 128
IN_DIM = 3 * CONV_DIM + 3 * ATTN_DIM + N_HEADS
EPS = 1e-6

kernel_name = "hymba_conv_fox_macaron_layer"


def rms_norm(x, g):
    xf = x.astype(jnp.float32)
    y = xf * lax.rsqrt(jnp.mean(xf * xf, axis=-1, keepdims=True) + EPS)
    return (y * g.astype(jnp.float32)).astype(x.dtype)


def group_rms_norm(x, g, n_groups):
    b, l, c = x.shape
    xg = x.astype(jnp.float32).reshape(b, l, n_groups, c // n_groups)
    xg = xg * lax.rsqrt(jnp.mean(xg * xg, axis=-1, keepdims=True) + EPS)
    return (xg.reshape(b, l, c) * g.astype(jnp.float32)).astype(x.dtype)


def swiglu_ffn(h, w_gu, w_down):
    gate, up = jnp.split(h @ w_gu, 2, axis=-1)
    return (jax.nn.silu(gate) * up) @ w_down


def short_conv_mixer(b_gate, c_gate, hc, conv_w):
    u = c_gate * hc
    y = lax.conv_general_dilated(
        u, conv_w.astype(u.dtype)[:, None, :], window_strides=(1,),
        padding=[(CONV_K - 1, 0)], dimension_numbers=('NWC', 'WIO', 'NWC'),
        feature_group_count=CONV_DIM)
    return b_gate * y


def fox_block(q_blk, fq_blk, q_pos, k, v, fk, k_pos):
    s = jnp.einsum('bhqd,bhkd->bhqk', q_blk, k, preferred_element_type=jnp.float32) * (HEAD_DIM ** -0.5)
    s = s + fq_blk[..., :, None] - fk[..., None, :]
    s = jnp.where(k_pos[None, :] <= q_pos[:, None], s, -jnp.inf)
    p = jax.nn.softmax(s, axis=-1)
    return jnp.einsum('bhqk,bhkd->bhqd', p.astype(v.dtype), v)


def forgetting_attention(q, k, v, fg_logit, b_f):
    bsz, L = q.shape[0], q.shape[1]
    log_f = jax.nn.log_sigmoid(fg_logit.astype(jnp.float32) + b_f.astype(jnp.float32))
    F = jnp.cumsum(log_f, axis=1).transpose(0, 2, 1)
    q, k, v = (t.transpose(0, 2, 1, 3) for t in (q, k, v))
    pos = jnp.arange(L)
    o_meta = fox_block(q[:, :, :N_META], F[:, :, :N_META], pos[:N_META],
                       k[:, :, :N_META], v[:, :, :N_META], F[:, :, :N_META], pos[:N_META])
    n_blk = (L - N_META) // Q_BLOCK
    qr = q[:, :, N_META:].reshape(bsz, N_HEADS, n_blk, Q_BLOCK, HEAD_DIM).transpose(2, 0, 1, 3, 4)
    fr = F[:, :, N_META:].reshape(bsz, N_HEADS, n_blk, Q_BLOCK).transpose(2, 0, 1, 3)
    pr = pos[N_META:].reshape(n_blk, Q_BLOCK)
    o_real = lax.map(lambda a: fox_block(a[0], a[1], a[2], k, v, F, pos), (qr, fr, pr))
    o_real = o_real.transpose(1, 2, 0, 3, 4).reshape(bsz, N_HEADS, L - N_META, HEAD_DIM)
    o = jnp.concatenate([o_meta, o_real], axis=2)
    return o.transpose(0, 2, 1, 3).reshape(bsz, L, ATTN_DIM)


def hybrid_mixer(h, w_in, conv_w, b_f, g_conv, g_attn, w_out):
    bsz, L, _ = h.shape
    proj = h @ w_in
    c0 = 3 * CONV_DIM
    b_gate, c_gate, hc, q, k, v, fg = jnp.split(
        proj, [CONV_DIM, 2 * CONV_DIM, c0, c0 + ATTN_DIM, c0 + 2 * ATTN_DIM, c0 + 3 * ATTN_DIM], axis=-1)
    y_conv = short_conv_mixer(b_gate, c_gate, hc, conv_w)
    hs = (bsz, L, N_HEADS, HEAD_DIM)
    y_attn = forgetting_attention(q.reshape(hs), k.reshape(hs), v.reshape(hs), fg, b_f)
    y = jnp.concatenate([group_rms_norm(y_conv, g_conv, N_CONV_GROUPS),
                         group_rms_norm(y_attn, g_attn, N_HEADS)], axis=-1)
    return y @ w_out


def setup_inputs(seed: int = 0) -> dict:
    key = jax.random.key(seed)
    ks = jax.random.split(key, 20)
    nrm = lambda k, shape, scale: jax.random.normal(k, shape, jnp.float32) * scale
    gain = lambda k, shape: 1.0 + 0.02 * jax.random.normal(k, shape, jnp.float32)
    return {
        'x': nrm(ks[0], (BATCH, SEQ, D_MODEL), 1.0),
        'meta_tokens': nrm(ks[1], (N_META, D_MODEL), 1.0),
        'ffn1_norm': gain(ks[2], (DEPTH, D_MODEL)),
        'ffn1_w_gu': nrm(ks[3], (DEPTH, D_MODEL, 2 * D_FF), D_MODEL ** -0.5),
        'ffn1_w_down': nrm(ks[4], (DEPTH, D_FF, D_MODEL), D_FF ** -0.5),
        'mix_norm': gain(ks[5], (DEPTH, D_MODEL)),
        'w_in': nrm(ks[6], (DEPTH, D_MODEL, IN_DIM), D_MODEL ** -0.5),
        'conv_w': nrm(ks[7], (DEPTH, CONV_K, CONV_DIM), CONV_K ** -0.5),
        'b_f': jnp.linspace(1.0, 6.0, N_HEADS, dtype=jnp.float32)[None, :] + nrm(ks[8], (DEPTH, N_HEADS), 0.1),
        'out_norm_conv': gain(ks[9], (DEPTH, CONV_DIM)),
        'out_norm_attn': gain(ks[10], (DEPTH, ATTN_DIM)),
        'w_out': nrm(ks[11], (DEPTH, MIX_WIDTH, D_MODEL), MIX_WIDTH ** -0.5),
        'ffn2_norm': gain(ks[12], (DEPTH, D_MODEL)),
        'ffn2_w_gu': nrm(ks[13], (DEPTH, D_MODEL, 2 * D_FF), D_MODEL ** -0.5),
        'ffn2_w_down': nrm(ks[14], (DEPTH, D_FF, D_MODEL), D_FF ** -0.5),
        'final_norm': gain(ks[15], (D_MODEL,)),
    }


def reference(x, meta_tokens, ffn1_norm, ffn1_w_gu, ffn1_w_down, mix_norm, w_in, conv_w, b_f,
              out_norm_conv, out_norm_attn, w_out, ffn2_norm, ffn2_w_gu, ffn2_w_down, final_norm):
    bsz = x.shape[0]
    meta = jnp.broadcast_to(meta_tokens.astype(x.dtype)[None], (bsz, N_META, D_MODEL))
    h = jnp.concatenate([meta, x], axis=1)
    for l in range(DEPTH):
        h = h + 0.5 * swiglu_ffn(rms_norm(h, ffn1_norm[l]), ffn1_w_gu[l], ffn1_w_down[l])
        h = h + hybrid_mixer(rms_norm(h, mix_norm[l]), w_in[l], conv_w[l], b_f[l],
                             out_norm_conv[l], out_norm_attn[l], w_out[l])
        h = h + 0.5 * swiglu_ffn(rms_norm(h, ffn2_norm[l]), ffn2_w_gu[l], ffn2_w_down[l])
    h = h[:, N_META:]
    return rms_norm(h, final_norm)
```

```python
import functools

import jax
import jax.numpy as jnp
from jax import lax
from jax.experimental import pallas as pl
from jax.experimental.pallas import tpu as pltpu

EPS = 1e-6
N_META = 16
HEAD_DIM = 64
CONV_K = 3

V7X_LANES = 128
V7X_SUBLANES = 8
V7X_MXU_DIM = 256
V7X_VMEM_BYTES = 64 * 1024 * 1024

TOKEN_TILE = 512
ATTN_TILE = V7X_MXU_DIM
FF_CHUNK = V7X_MXU_DIM
HEAD_PAIR = 2 * HEAD_DIM

F32 = jnp.float32
BF16 = jnp.bfloat16


def _mxu_dot(a, b):
    return jnp.dot(a, b, preferred_element_type=F32)


def _rms_norm(x, g):
    return x * lax.rsqrt(jnp.mean(x * x, axis=-1, keepdims=True) + EPS) * g


def _swiglu_ffn(n, wgu_ref, wdown_ref, act_scr):
    d_ff = wdown_ref.shape[0]
    for lo in range(0, d_ff, FF_CHUNK):
        g = _mxu_dot(n, wgu_ref[:, lo:lo + FF_CHUNK])
        u = _mxu_dot(n, wgu_ref[:, d_ff + lo:d_ff + lo + FF_CHUNK])
        act_scr[:, lo:lo + FF_CHUNK] = (g * jax.nn.sigmoid(g) * u).astype(BF16)
    return _mxu_dot(act_scr[...], wdown_ref[...])


def _group_norm_lanes(y, g):
    lane = lax.broadcasted_iota(jnp.int32, (y.shape[0], V7X_LANES), 1)
    low = lane < HEAD_DIM
    outs = []
    for c in range(0, y.shape[1], V7X_LANES):
        blk = y[:, c:c + V7X_LANES]
        sq = blk * blk
        ms_lo = jnp.sum(jnp.where(low, sq, 0.0), axis=-1, keepdims=True) * (1.0 / HEAD_DIM)
        ms_hi = jnp.sum(jnp.where(low, 0.0, sq), axis=-1, keepdims=True) * (1.0 / HEAD_DIM)
        inv = jnp.where(low, lax.rsqrt(ms_lo + EPS), lax.rsqrt(ms_hi + EPS))
        outs.append(blk * inv * g[:, c:c + V7X_LANES])
    return jnp.concatenate(outs, axis=-1)


def _log_sigmoid(x):
    z = -x
    return -(jnp.maximum(z, 0.0) + jnp.log1p(jnp.exp(-jnp.abs(z))))


def _cumsum_lanes(a):
    r, n = a.shape
    lane = lax.broadcasted_iota(jnp.int32, (r, V7X_LANES), 1)
    blocks = []
    offset = jnp.zeros((r, 1), F32)
    for c in range(0, n, V7X_LANES):
        blk = a[:, c:c + V7X_LANES]
        shift = 1
        while shift < V7X_LANES:
            blk = blk + jnp.where(lane >= shift, pltpu.roll(blk, shift, axis=1), 0.0)
            shift *= 2
        blk = blk + offset
        offset = blk[:, V7X_LANES - 1:V7X_LANES]
        blocks.append(blk)
    return jnp.concatenate(blocks, axis=-1)


def _pre_mixer_body(x_ref, g1_ref, wgu_ref, wdown_ref, gm_ref, win_ref, wfg_ref, bf_ref, convw_ref,
                    gconv_ref, uinit_ref,
                    h1_ref, q_ref, k_ref, vt_ref, yconv_ref, gcol_ref, utail_ref,
                    act_scr, u_scr, gcarry_scr, *, tiles_per_seq, tail_start):
    tm = x_ref.shape[0]
    cw = q_ref.shape[1]
    halo = V7X_SUBLANES

    @pl.when(pl.program_id(0) % tiles_per_seq == 0)
    def _():
        u_scr[0:halo, :] = uinit_ref[...]
        gcarry_scr[...] = jnp.zeros_like(gcarry_scr)

    x = x_ref[...]
    n1 = _rms_norm(x, g1_ref[...]).astype(BF16)
    h1 = x + 0.5 * _swiglu_ffn(n1, wgu_ref, wdown_ref, act_scr)
    h1_ref[...] = h1
    n2 = _rms_norm(h1, gm_ref[...]).astype(BF16)

    def proj(j):
        return _mxu_dot(n2, win_ref[:, j * cw:(j + 1) * cw])

    u = proj(1) * proj(2)
    u_scr[halo:halo + tm, :] = u
    conv = (convw_ref[0:1, :] * u_scr[pl.ds(halo - 2, tm), :]
            + convw_ref[1:2, :] * u_scr[pl.ds(halo - 1, tm), :]
            + convw_ref[2:3, :] * u)
    yconv_ref[...] = _group_norm_lanes(proj(0) * conv, gconv_ref[...]).astype(BF16)
    utail_ref[...] = u[tail_start:tail_start + halo, :]
    u_scr[0:halo, :] = u[tm - halo:tm, :]

    q_ref[...] = (proj(3) * (HEAD_DIM ** -0.5)).astype(BF16)
    k_ref[...] = proj(4).astype(BF16)
    vt = proj(5).T
    tk = vt_ref.shape[2]
    for c in range(vt_ref.shape[0]):
        vt_ref[c] = vt[:, c * tk:(c + 1) * tk].astype(BF16)

    fg = _mxu_dot(n2, wfg_ref[...]) + bf_ref[...]
    g_run = _cumsum_lanes(_log_sigmoid(fg).T) + gcarry_scr[:, 0:1]
    gcarry_scr[...] = jnp.broadcast_to(g_run[:, tm - 1:tm], gcarry_scr.shape)
    gcol_ref[...] = g_run.T


def _pre_mixer_call(x2, g1, wgu, wdown, gm, win, wfg, bfp, convw, gconv, uinit, *, tm, tiles_per_seq,
                    tail_start, tk):
    n_rows, d = x2.shape
    cw = gconv.shape[1]
    d_ff = wdown.shape[0]
    n_tiles = n_rows // tm
    halo = V7X_SUBLANES

    def resident(shape):
        return pl.BlockSpec(shape, lambda t: (0,) * len(shape), pipeline_mode=pl.Buffered(1))

    def rows(width):
        return pl.BlockSpec((tm, width), lambda t: (t, 0))

    body = functools.partial(_pre_mixer_body, tiles_per_seq=tiles_per_seq, tail_start=tail_start)
    return pl.pallas_call(
        body,
        grid=(n_tiles,),
        in_specs=[rows(d), resident(g1.shape), resident(wgu.shape), resident(wdown.shape),
                  resident(gm.shape), resident(win.shape), resident(wfg.shape), resident(bfp.shape),
                  resident(convw.shape), resident(gconv.shape), resident(uinit.shape)],
        out_specs=[rows(d), rows(cw), rows(cw),
                   pl.BlockSpec((tm // tk, cw, tk), lambda t: (t, 0, 0)),
                   rows(cw), rows(V7X_LANES),
                   pl.BlockSpec((halo, cw), lambda t: (t, 0))],
        out_shape=[jax.ShapeDtypeStruct((n_rows, d), F32),
                   jax.ShapeDtypeStruct((n_rows, cw), BF16),
                   jax.ShapeDtypeStruct((n_rows, cw), BF16),
                   jax.ShapeDtypeStruct((n_rows // tk, cw, tk), BF16),
                   jax.ShapeDtypeStruct((n_rows, cw), BF16),
                   jax.ShapeDtypeStruct((n_rows, V7X_LANES), F32),
                   jax.ShapeDtypeStruct((n_tiles * halo, cw), F32)],
        scratch_shapes=[pltpu.VMEM((tm, d_ff), BF16),
                        pltpu.VMEM((tm + halo, cw), F32),
                        pltpu.VMEM((V7X_LANES, V7X_LANES), F32)],
        compiler_params=pltpu.CompilerParams(dimension_semantics=("arbitrary",),
                                             vmem_limit_bytes=_vmem_limit()),
        name="pre_mixer",
    )(x2, g1, wgu, wdown, gm, win, wfg, bfp, convw, gconv, uinit)


def _attention_body(q_ref, k_ref, vt_ref, g_ref, km_ref, vtm_ref, gm_ref, gattn_ref, y_ref, ot_scr):
    tq = q_ref.shape[0]
    tk = vt_ref.shape[2]
    n_heads = q_ref.shape[1] // HEAD_DIM
    i = pl.program_id(1)
    neg_inf = float("-inf")

    qt = q_ref[...].astype(F32).T.astype(BF16)
    pair_row = lax.broadcasted_iota(jnp.int32, (HEAD_PAIR, tq), 0)
    key_pos = lax.broadcasted_iota(jnp.int32, (tk, tq), 0)
    qry_pos = lax.broadcasted_iota(jnp.int32, (tk, tq), 1)
    causal = key_pos <= qry_pos
    meta_valid = lax.broadcasted_iota(jnp.int32, (km_ref.shape[0], tq), 0) < N_META
    g_meta = gm_ref[...]
    g_meta = g_meta - g_meta[N_META - 1:N_META, :]

    for h in range(n_heads):
        pair, half = divmod(h, 2)
        lanes = slice(pair * HEAD_PAIR, (pair + 1) * HEAD_PAIR)
        head_rows = slice(h * HEAD_DIM, (h + 1) * HEAD_DIM)
        mine = (pair_row >= half * HEAD_DIM) & (pair_row < (half + 1) * HEAD_DIM)
        rhs = jnp.where(mine, qt[lanes, :], jnp.zeros((), BF16))

        def scores(k_blk, g_blk):
            s = _mxu_dot(k_blk, rhs)
            return s - g_blk[:, h:h + 1]

        def online_step(s, carry, vt_blk):
            m, l, acc = carry
            m_new = jnp.maximum(m, jnp.max(s, axis=0, keepdims=True))
            alpha = jnp.exp(m - m_new)
            p = jnp.exp(s - m_new)
            l = alpha * l + jnp.sum(p, axis=0, keepdims=True)
            acc = alpha * acc + _mxu_dot(vt_blk, p.astype(BF16))
            return m_new, l, acc

        s = jnp.where(meta_valid, scores(km_ref[:, lanes], g_meta), neg_inf)
        m = jnp.max(s, axis=0, keepdims=True)
        p = jnp.exp(s - m)
        carry = (m, jnp.sum(p, axis=0, keepdims=True),
                 _mxu_dot(vtm_ref[head_rows, :], p.astype(BF16)))

        def full_chunk(j, carry):
            off = pl.multiple_of(j * tk, tk)
            s = scores(k_ref[pl.ds(off, tk), lanes], g_ref[pl.ds(off, tk), :])
            return online_step(s, carry, vt_ref[j, head_rows, :])

        carry = lax.fori_loop(0, i, full_chunk, carry)

        off = pl.multiple_of(i * tk, tk)
        s = jnp.where(causal, scores(k_ref[pl.ds(off, tk), lanes], g_ref[pl.ds(off, tk), :]), neg_inf)
        _, l, acc = online_step(s, carry, vt_ref[i, head_rows, :])

        o = acc / l
        ot_scr[head_rows, :] = o * lax.rsqrt(jnp.mean(o * o, axis=0, keepdims=True) + EPS)

    y_ref[...] = (ot_scr[...].T * gattn_ref[...]).astype(BF16)


def _attention_call(q, k, vt, gcol, k_meta, vt_meta, g_meta, gattn, *, batch, seq, tq):
    cw = q.shape[1]
    tk = vt.shape[2]
    assert tq == tk, "the diagonal chunk assumes query tile == key chunk"
    chunks = seq // tk
    vt4 = vt.reshape(batch, chunks, cw, tk)
    q_tiles = seq // tq

    def const(shape):
        return pl.BlockSpec(shape, lambda b, i: (0,) * len(shape))

    return pl.pallas_call(
        _attention_body,
        grid=(batch, q_tiles),
        in_specs=[pl.BlockSpec((tq, cw), lambda b, i: (b * q_tiles + i, 0)),
                  pl.BlockSpec((seq, cw), lambda b, i: (b, 0)),
                  pl.BlockSpec((None, chunks, cw, tk), lambda b, i: (b, 0, 0, 0)),
                  pl.BlockSpec((seq, V7X_LANES), lambda b, i: (b, 0)),
                  const(k_meta.shape), const(vt_meta.shape), const(g_meta.shape), const(gattn.shape)],
        out_specs=pl.BlockSpec((tq, cw), lambda b, i: (b * q_tiles + i, 0)),
        out_shape=jax.ShapeDtypeStruct(q.shape, BF16),
        scratch_shapes=[pltpu.VMEM((cw, tq), F32)],
        compiler_params=pltpu.CompilerParams(dimension_semantics=("parallel", "parallel"),
                                             vmem_limit_bytes=_vmem_limit()),
        name="fox_attention",
    )(q, k, vt4, gcol, k_meta, vt_meta, g_meta, gattn)


def _post_mixer_body(h1_ref, yc_ref, ya_ref, wout_ref, g2_ref, wgu_ref, wdown_ref, gf_ref, out_ref, act_scr):
    cw = yc_ref.shape[1]
    h2 = (h1_ref[...]
          + _mxu_dot(yc_ref[...], wout_ref[0:cw, :])
          + _mxu_dot(ya_ref[...], wout_ref[cw:2 * cw, :]))
    n = _rms_norm(h2, g2_ref[...]).astype(BF16)
    h3 = h2 + 0.5 * _swiglu_ffn(n, wgu_ref, wdown_ref, act_scr)
    out_ref[...] = _rms_norm(h3, gf_ref[...])


def _post_mixer_call(h1, yconv, yattn, wout, g2, wgu, wdown, gf, *, tm):
    n_rows, d = h1.shape
    cw = yconv.shape[1]
    d_ff = wdown.shape[0]

    def resident(shape):
        return pl.BlockSpec(shape, lambda t: (0,) * len(shape), pipeline_mode=pl.Buffered(1))

    def rows(width):
        return pl.BlockSpec((tm, width), lambda t: (t, 0))

    return pl.pallas_call(
        _post_mixer_body,
        grid=(n_rows // tm,),
        in_specs=[rows(d), rows(cw), rows(cw), resident(wout.shape), resident(g2.shape),
                  resident(wgu.shape), resident(wdown.shape), resident(gf.shape)],
        out_specs=rows(d),
        out_shape=jax.ShapeDtypeStruct((n_rows, d), F32),
        scratch_shapes=[pltpu.VMEM((tm, d_ff), BF16)],
        compiler_params=pltpu.CompilerParams(dimension_semantics=("parallel",),
                                             vmem_limit_bytes=_vmem_limit()),
        name="post_mixer",
    )(h1, yconv, yattn, wout, g2, wgu, wdown, gf)


def _vmem_limit():
    return V7X_VMEM_BYTES * 7 // 8


def _layer(x, meta_tokens, ffn1_norm, ffn1_w_gu, ffn1_w_down, mix_norm, w_in, conv_w, b_f, out_norm_conv,
           out_norm_attn, w_out, ffn2_norm, ffn2_w_gu, ffn2_w_down, final_norm, *, tm, tq):
    batch, seq, d = x.shape
    assert ffn1_norm.shape[0] == 1, "one layer deep"
    assert seq % tm == 0 and tm % tq == 0
    cw = conv_w.shape[2]
    n_heads = b_f.shape[1]
    row = lambda a: a.reshape(1, -1).astype(F32)

    wgu1, wdown1 = ffn1_w_gu[0].astype(BF16), ffn1_w_down[0].astype(BF16)
    wgu2, wdown2 = ffn2_w_gu[0].astype(BF16), ffn2_w_down[0].astype(BF16)
    win = w_in[0, :, :6 * cw].astype(BF16)
    wfg = jnp.pad(w_in[0, :, 6 * cw:], ((0, 0), (0, V7X_LANES - n_heads))).astype(BF16)
    bfp = jnp.pad(row(b_f[0]), ((0, 0), (0, V7X_LANES - n_heads)))
    wout = w_out[0].astype(BF16)
    pre_args = (row(ffn1_norm[0]), wgu1, wdown1, row(mix_norm[0]), win, wfg, bfp, conv_w[0].astype(F32),
                row(out_norm_conv[0]))

    meta_rows = V7X_LANES
    x_meta = jnp.pad(meta_tokens.astype(F32), ((0, meta_rows - N_META), (0, 0)))
    _, _, k_meta, vt_meta, _, g_meta, u_meta = _pre_mixer_call(
        x_meta, *pre_args, jnp.zeros((V7X_SUBLANES, cw), F32),
        tm=meta_rows, tiles_per_seq=1, tail_start=N_META - V7X_SUBLANES, tk=meta_rows)

    h1, q, k, vt, yconv, gcol, _ = _pre_mixer_call(
        x.reshape(batch * seq, d), *pre_args, u_meta,
        tm=tm, tiles_per_seq=seq // tm, tail_start=tm - V7X_SUBLANES, tk=tq)

    yattn = _attention_call(q, k, vt, gcol, k_meta, vt_meta.reshape(cw, meta_rows), g_meta,
                            row(out_norm_attn[0]), batch=batch, seq=seq, tq=tq)

    out = _post_mixer_call(h1, yconv, yattn, wout, row(ffn2_norm[0]), wgu2, wdown2, row(final_norm), tm=tm)
    return out.reshape(batch, seq, d)


def kernel(x, meta_tokens, ffn1_norm, ffn1_w_gu, ffn1_w_down, mix_norm, w_in, conv_w, b_f, out_norm_conv,
           out_norm_attn, w_out, ffn2_norm, ffn2_w_gu, ffn2_w_down, final_norm):
    return _layer(x, meta_tokens, ffn1_norm, ffn1_w_gu, ffn1_w_down, mix_norm, w_in, conv_w, b_f,
                  out_norm_conv, out_norm_attn, w_out, ffn2_norm, ffn2_w_gu, ffn2_w_down, final_norm,
                  tm=TOKEN_TILE, tq=ATTN_TILE)
```

```python
import functools

import jax
import jax.numpy as jnp
from jax import lax
from jax.experimental import pallas as pl
from jax.experimental.pallas import tpu as pltpu

EPS = 1e-6
N_META = 16
HEAD_DIM = 64
CONV_K = 3

V7X_LANES = 128
V7X_SUBLANES = 8
V7X_MXU_DIM = 256
V7X_VMEM_BYTES = 64 * 1024 * 1024

TOKEN_TILE = 512
ATTN_TILE = V7X_MXU_DIM
FF_CHUNK = V7X_MXU_DIM
HEAD_PAIR = 2 * HEAD_DIM

F32 = jnp.float32
BF16 = jnp.bfloat16


def _mxu_dot(a, b):
    return jnp.dot(a, b, preferred_element_type=F32)


def _rms_norm(x, g):
    return x * lax.rsqrt(jnp.mean(x * x, axis=-1, keepdims=True) + EPS) * g


def _swiglu_ffn(n, wgu_ref, wdown_ref, act_scr):
    d_ff = wdown_ref.shape[0]
    for lo in range(0, d_ff, FF_CHUNK):
        g = _mxu_dot(n, wgu_ref[:, lo:lo + FF_CHUNK])
        u = _mxu_dot(n, wgu_ref[:, d_ff + lo:d_ff + lo + FF_CHUNK])
        act_scr[:, lo:lo + FF_CHUNK] = (g * jax.nn.sigmoid(g) * u).astype(BF16)
    return _mxu_dot(act_scr[...], wdown_ref[...])


def _group_norm_lanes(y, g):
    lane = lax.broadcasted_iota(jnp.int32, (y.shape[0], V7X_LANES), 1)
    low = lane < HEAD_DIM
    outs = []
    for c in range(0, y.shape[1], V7X_LANES):
        blk = y[:, c:c + V7X_LANES]
        sq = blk * blk
        ms_lo = jnp.sum(jnp.where(low, sq, 0.0), axis=-1, keepdims=True) * (1.0 / HEAD_DIM)
        ms_hi = jnp.sum(jnp.where(low, 0.0, sq), axis=-1, keepdims=True) * (1.0 / HEAD_DIM)
        inv = jnp.where(low, lax.rsqrt(ms_lo + EPS), lax.rsqrt(ms_hi + EPS))
        outs.append(blk * inv * g[:, c:c + V7X_LANES])
    return jnp.concatenate(outs, axis=-1)


def _log_sigmoid(x):
    z = -x
    return -(jnp.maximum(z, 0.0) + jnp.log1p(jnp.exp(-jnp.abs(z))))


def _cumsum_lanes(a):
    r, n = a.shape
    lane = lax.broadcasted_iota(jnp.int32, (r, V7X_LANES), 1)
    blocks = []
    offset = jnp.zeros((r, 1), F32)
    for c in range(0, n, V7X_LANES):
        blk = a[:, c:c + V7X_LANES]
        shift = 1
        while shift < V7X_LANES:
            blk = blk + jnp.where(lane >= shift, pltpu.roll(blk, shift, axis=1), 0.0)
            shift *= 2
        blk = blk + offset
        offset = blk[:, V7X_LANES - 1:V7X_LANES]
        blocks.append(blk)
    return jnp.concatenate(blocks, axis=-1)


def _pre_mixer_body(x_ref, g1_ref, wgu_ref, wdown_ref, gm_ref, win_ref, wfg_ref, bf_ref, convw_ref,
                    gconv_ref, uinit_ref,
                    h1_ref, q_ref, k_ref, vt_ref, yconv_ref, gcol_ref, utail_ref,
                    act_scr, u_scr, gcarry_scr, *, tiles_per_seq, tail_start):
    tm = x_ref.shape[0]
    cw = q_ref.shape[1]
    halo = V7X_SUBLANES

    @pl.when(pl.program_id(0) % tiles_per_seq == 0)
    def _():
        u_scr[0:halo, :] = uinit_ref[...]
        gcarry_scr[...] = jnp.zeros_like(gcarry_scr)

    x = x_ref[...]
    n1 = _rms_norm(x, g1_ref[...]).astype(BF16)
    h1 = x + 0.5 * _swiglu_ffn(n1, wgu_ref, wdown_ref, act_scr)
    h1_ref[...] = h1
    n2 = _rms_norm(h1, gm_ref[...]).astype(BF16)

    def proj(j):
        return _mxu_dot(n2, win_ref[:, j * cw:(j + 1) * cw])

    u = proj(1) * proj(2)
    u_scr[halo:halo + tm, :] = u
    conv = (convw_ref[0:1, :] * u_scr[pl.ds(halo - 2, tm), :]
            + convw_ref[1:2, :] * u_scr[pl.ds(halo - 1, tm), :]
            + convw_ref[2:3, :] * u)
    yconv_ref[...] = _group_norm_lanes(proj(0) * conv, gconv_ref[...]).astype(BF16)
    utail_ref[...] = u[tail_start:tail_start + halo, :]
    u_scr[0:halo, :] = u[tm - halo:tm, :]

    q_ref[...] = (proj(3) * (HEAD_DIM ** -0.5)).astype(BF16)
    k_ref[...] = proj(4).astype(BF16)
    vt = proj(5).T
    tk = vt_ref.shape[2]
    for c in range(vt_ref.shape[0]):
        vt_ref[c] = vt[:, c * tk:(c + 1) * tk].astype(BF16)

    fg = _mxu_dot(n2, wfg_ref[...]) + bf_ref[...]
    g_run = _cumsum_lanes(_log_sigmoid(fg).T) + gcarry_scr[:, 0:1]
    gcarry_scr[...] = jnp.broadcast_to(g_run[:, tm - 1:tm], gcarry_scr.shape)
    gcol_ref[...] = g_run.T


def _pre_mixer_call(x2, g1, wgu, wdown, gm, win, wfg, bfp, convw, gconv, uinit, *, tm, tiles_per_seq,
                    tail_start, tk):
    n_rows, d = x2.shape
    cw = gconv.shape[1]
    d_ff = wdown.shape[0]
    n_tiles = n_rows // tm
    halo = V7X_SUBLANES

    def resident(shape):
        return pl.BlockSpec(shape, lambda t: (0,) * len(shape), pipeline_mode=pl.Buffered(1))

    def rows(width):
        return pl.BlockSpec((tm, width), lambda t: (t, 0))

    body = functools.partial(_pre_mixer_body, tiles_per_seq=tiles_per_seq, tail_start=tail_start)
    return pl.pallas_call(
        body,
        grid=(n_tiles,),
        in_specs=[rows(d), resident(g1.shape), resident(wgu.shape), resident(wdown.shape),
                  resident(gm.shape), resident(win.shape), resident(wfg.shape), resident(bfp.shape),
                  resident(convw.shape), resident(gconv.shape), resident(uinit.shape)],
        out_specs=[rows(d), rows(cw), rows(cw),
                   pl.BlockSpec((tm // tk, cw, tk), lambda t: (t, 0, 0)),
                   rows(cw), rows(V7X_LANES),
                   pl.BlockSpec((halo, cw), lambda t: (t, 0))],
        out_shape=[jax.ShapeDtypeStruct((n_rows, d), F32),
                   jax.ShapeDtypeStruct((n_rows, cw), BF16),
                   jax.ShapeDtypeStruct((n_rows, cw), BF16),
                   jax.ShapeDtypeStruct((n_rows // tk, cw, tk), BF16),
                   jax.ShapeDtypeStruct((n_rows, cw), BF16),
                   jax.ShapeDtypeStruct((n_rows, V7X_LANES), F32),
                   jax.ShapeDtypeStruct((n_tiles * halo, cw), F32)],
        scratch_shapes=[pltpu.VMEM((tm, d_ff), BF16),
                        pltpu.VMEM((tm + halo, cw), F32),
                        pltpu.VMEM((V7X_LANES, V7X_LANES), F32)],
        compiler_params=pltpu.CompilerParams(dimension_semantics=("arbitrary",),
                                             vmem_limit_bytes=_vmem_limit()),
        name="pre_mixer",
    )(x2, g1, wgu, wdown, gm, win, wfg, bfp, convw, gconv, uinit)


def _attention_body(q_ref, k_ref, vt_ref, g_ref, km_ref, vtm_ref, gm_ref, gattn_ref, y_ref,
                    qm_scr, m_scr, l_scr, acc_scr):
    tq = q_ref.shape[0]
    tk = vt_ref.shape[2]
    n_heads = q_ref.shape[1] // HEAD_DIM
    i = pl.program_id(1)
    neg_inf = float("-inf")

    qt = q_ref[...].astype(F32).T.astype(BF16)
    pair_row = lax.broadcasted_iota(jnp.int32, (HEAD_PAIR, tq), 0)
    for h in range(n_heads):
        pair, half = divmod(h, 2)
        mine = (pair_row >= half * HEAD_DIM) & (pair_row < (half + 1) * HEAD_DIM)
        qm_scr[h] = jnp.where(mine, qt[pair * HEAD_PAIR:(pair + 1) * HEAD_PAIR, :], jnp.zeros((), BF16))

    def chunk_update(k_rows, g_blk, vt_blk, mask, first):
        scores = [_mxu_dot(k_rows[:, (h // 2) * HEAD_PAIR:(h // 2 + 1) * HEAD_PAIR], qm_scr[h])
                  for h in range(n_heads)]
        probs, alphas = [], []
        for h in range(n_heads):
            s = scores[h] - g_blk[:, h:h + 1]
            if mask is not None:
                s = jnp.where(mask, s, neg_inf)
            c_max = jnp.max(s, axis=0, keepdims=True)
            if first:
                m_new = c_max
            else:
                m_old = m_scr[h:h + 1, :]
                m_new = jnp.maximum(m_old, c_max)
                alphas.append(jnp.exp(m_old - m_new))
            p = jnp.exp(s - m_new)
            p_sum = jnp.sum(p, axis=0, keepdims=True)
            probs.append(p.astype(BF16))
            m_scr[h:h + 1, :] = m_new
            l_scr[h:h + 1, :] = p_sum if first else alphas[h] * l_scr[h:h + 1, :] + p_sum
        for h in range(n_heads):
            head_rows = slice(h * HEAD_DIM, (h + 1) * HEAD_DIM)
            pv = _mxu_dot(vt_blk[head_rows, :], probs[h])
            acc_scr[head_rows, :] = pv if first else alphas[h] * acc_scr[head_rows, :] + pv

    g_meta = gm_ref[...]
    g_meta = g_meta - g_meta[N_META - 1:N_META, :]
    meta_valid = lax.broadcasted_iota(jnp.int32, (km_ref.shape[0], tq), 0) < N_META
    chunk_update(km_ref, g_meta, vtm_ref, meta_valid, first=True)

    def full_chunk(j, carry):
        off = pl.multiple_of(j * tk, tk)
        chunk_update(k_ref.at[pl.ds(off, tk), :], g_ref[pl.ds(off, tk), :], vt_ref.at[j], None, first=False)
        return carry

    lax.fori_loop(0, i, full_chunk, 0)

    causal = (lax.broadcasted_iota(jnp.int32, (tk, tq), 0)
              <= lax.broadcasted_iota(jnp.int32, (tk, tq), 1))
    off = pl.multiple_of(i * tk, tk)
    chunk_update(k_ref.at[pl.ds(off, tk), :], g_ref[pl.ds(off, tk), :], vt_ref.at[i], causal, first=False)

    for h in range(n_heads):
        head_rows = slice(h * HEAD_DIM, (h + 1) * HEAD_DIM)
        o = acc_scr[head_rows, :] / l_scr[h:h + 1, :]
        acc_scr[head_rows, :] = o * lax.rsqrt(jnp.mean(o * o, axis=0, keepdims=True) + EPS)
    y_ref[...] = (acc_scr[...].T * gattn_ref[...]).astype(BF16)


def _attention_call(q, k, vt, gcol, k_meta, vt_meta, g_meta, gattn, *, batch, seq, tq):
    cw = q.shape[1]
    tk = vt.shape[2]
    assert tq == tk, "the diagonal chunk assumes query tile == key chunk"
    chunks = seq // tk
    n_heads = cw // HEAD_DIM
    vt4 = vt.reshape(batch, chunks, cw, tk)
    q_tiles = seq // tq

    def const(shape):
        return pl.BlockSpec(shape, lambda b, i: (0,) * len(shape))

    return pl.pallas_call(
        _attention_body,
        grid=(batch, q_tiles),
        in_specs=[pl.BlockSpec((tq, cw), lambda b, i: (b * q_tiles + i, 0)),
                  pl.BlockSpec((seq, cw), lambda b, i: (b, 0)),
                  pl.BlockSpec((None, chunks, cw, tk), lambda b, i: (b, 0, 0, 0)),
                  pl.BlockSpec((seq, V7X_LANES), lambda b, i: (b, 0)),
                  const(k_meta.shape), const(vt_meta.shape), const(g_meta.shape), const(gattn.shape)],
        out_specs=pl.BlockSpec((tq, cw), lambda b, i: (b * q_tiles + i, 0)),
        out_shape=jax.ShapeDtypeStruct(q.shape, BF16),
        scratch_shapes=[pltpu.VMEM((n_heads, HEAD_PAIR, tq), BF16),
                        pltpu.VMEM((n_heads, tq), F32),
                        pltpu.VMEM((n_heads, tq), F32),
                        pltpu.VMEM((cw, tq), F32)],
        compiler_params=pltpu.CompilerParams(dimension_semantics=("parallel", "parallel"),
                                             vmem_limit_bytes=_vmem_limit()),
        name="fox_attention",
    )(q, k, vt4, gcol, k_meta, vt_meta, g_meta, gattn)


def _post_mixer_body(h1_ref, yc_ref, ya_ref, wout_ref, g2_ref, wgu_ref, wdown_ref, gf_ref, out_ref, act_scr):
    cw = yc_ref.shape[1]
    h2 = (h1_ref[...]
          + _mxu_dot(yc_ref[...], wout_ref[0:cw, :])
          + _mxu_dot(ya_ref[...], wout_ref[cw:2 * cw, :]))
    n = _rms_norm(h2, g2_ref[...]).astype(BF16)
    h3 = h2 + 0.5 * _swiglu_ffn(n, wgu_ref, wdown_ref, act_scr)
    out_ref[...] = _rms_norm(h3, gf_ref[...])


def _post_mixer_call(h1, yconv, yattn, wout, g2, wgu, wdown, gf, *, tm):
    n_rows, d = h1.shape
    cw = yconv.shape[1]
    d_ff = wdown.shape[0]

    def resident(shape):
        return pl.BlockSpec(shape, lambda t: (0,) * len(shape), pipeline_mode=pl.Buffered(1))

    def rows(width):
        return pl.BlockSpec((tm, width), lambda t: (t, 0))

    return pl.pallas_call(
        _post_mixer_body,
        grid=(n_rows // tm,),
        in_specs=[rows(d), rows(cw), rows(cw), resident(wout.shape), resident(g2.shape),
                  resident(wgu.shape), resident(wdown.shape), resident(gf.shape)],
        out_specs=rows(d),
        out_shape=jax.ShapeDtypeStruct((n_rows, d), F32),
        scratch_shapes=[pltpu.VMEM((tm, d_ff), BF16)],
        compiler_params=pltpu.CompilerParams(dimension_semantics=("parallel",),
                                             vmem_limit_bytes=_vmem_limit()),
        name="post_mixer",
    )(h1, yconv, yattn, wout, g2, wgu, wdown, gf)


def _vmem_limit():
    return V7X_VMEM_BYTES * 7 // 8


def _layer(x, meta_tokens, ffn1_norm, ffn1_w_gu, ffn1_w_down, mix_norm, w_in, conv_w, b_f, out_norm_conv,
           out_norm_attn, w_out, ffn2_norm, ffn2_w_gu, ffn2_w_down, final_norm, *, tm, tq):
    batch, seq, d = x.shape
    assert ffn1_norm.shape[0] == 1, "one layer deep"
    assert seq % tm == 0 and tm % tq == 0
    cw = conv_w.shape[2]
    n_heads = b_f.shape[1]
    row = lambda a: a.reshape(1, -1).astype(F32)

    wgu1, wdown1 = ffn1_w_gu[0].astype(BF16), ffn1_w_down[0].astype(BF16)
    wgu2, wdown2 = ffn2_w_gu[0].astype(BF16), ffn2_w_down[0].astype(BF16)
    win = w_in[0, :, :6 * cw].astype(BF16)
    wfg = jnp.pad(w_in[0, :, 6 * cw:], ((0, 0), (0, V7X_LANES - n_heads))).astype(BF16)
    bfp = jnp.pad(row(b_f[0]), ((0, 0), (0, V7X_LANES - n_heads)))
    wout = w_out[0].astype(BF16)
    pre_args = (row(ffn1_norm[0]), wgu1, wdown1, row(mix_norm[0]), win, wfg, bfp, conv_w[0].astype(F32),
                row(out_norm_conv[0]))

    meta_rows = V7X_LANES
    x_meta = jnp.pad(meta_tokens.astype(F32), ((0, meta_rows - N_META), (0, 0)))
    _, _, k_meta, vt_meta, _, g_meta, u_meta = _pre_mixer_call(
        x_meta, *pre_args, jnp.zeros((V7X_SUBLANES, cw), F32),
        tm=meta_rows, tiles_per_seq=1, tail_start=N_META - V7X_SUBLANES, tk=meta_rows)

    h1, q, k, vt, yconv, gcol, _ = _pre_mixer_call(
        x.reshape(batch * seq, d), *pre_args, u_meta,
        tm=tm, tiles_per_seq=seq // tm, tail_start=tm - V7X_SUBLANES, tk=tq)

    yattn = _attention_call(q, k, vt, gcol, k_meta, vt_meta.reshape(cw, meta_rows), g_meta,
                            row(out_norm_attn[0]), batch=batch, seq=seq, tq=tq)

    out = _post_mixer_call(h1, yconv, yattn, wout, row(ffn2_norm[0]), wgu2, wdown2, row(final_norm), tm=tm)
    return out.reshape(batch, seq, d)


def kernel(x, meta_tokens, ffn1_norm, ffn1_w_gu, ffn1_w_down, mix_norm, w_in, conv_w, b_f, out_norm_conv,
           out_norm_attn, w_out, ffn2_norm, ffn2_w_gu, ffn2_w_down, final_norm):
    return _layer(x, meta_tokens, ffn1_norm, ffn1_w_gu, ffn1_w_down, mix_norm, w_in, conv_w, b_f,
                  out_norm_conv, out_norm_attn, w_out, ffn2_norm, ffn2_w_gu, ffn2_w_down, final_norm,
                  tm=TOKEN_TILE, tq=ATTN_TILE)
```

```python
import functools
import math

import jax
import jax.numpy as jnp
from jax import lax
from jax.experimental import pallas as pl
from jax.experimental.pallas import tpu as pltpu

EPS = 1e-6
LOG2_E = math.log2(math.e)
N_META = 16
HEAD_DIM = 64
CONV_K = 3

V7X_LANES = 128
V7X_SUBLANES = 8
V7X_MXU_DIM = 256
V7X_VMEM_BYTES = 64 * 1024 * 1024

TOKEN_TILE = 512
ATTN_TILE = V7X_MXU_DIM
FF_CHUNK = V7X_MXU_DIM
HEAD_PAIR = 2 * HEAD_DIM

F32 = jnp.float32
BF16 = jnp.bfloat16


def _mxu_dot(a, b):
    return jnp.dot(a, b, preferred_element_type=F32)


def _rms_norm(x, g):
    return x * lax.rsqrt(jnp.mean(x * x, axis=-1, keepdims=True) + EPS) * g


def _swiglu_ffn(n, wgu_ref, wdown_ref, act_scr):
    d_ff = wdown_ref.shape[0]
    for lo in range(0, d_ff, FF_CHUNK):
        g = _mxu_dot(n, wgu_ref[:, lo:lo + FF_CHUNK])
        u = _mxu_dot(n, wgu_ref[:, d_ff + lo:d_ff + lo + FF_CHUNK])
        act_scr[:, lo:lo + FF_CHUNK] = (g * jax.nn.sigmoid(g) * u).astype(BF16)
    return _mxu_dot(act_scr[...], wdown_ref[...])


def _group_norm_lanes(y, g):
    lane = lax.broadcasted_iota(jnp.int32, (y.shape[0], V7X_LANES), 1)
    low = lane < HEAD_DIM
    outs = []
    for c in range(0, y.shape[1], V7X_LANES):
        blk = y[:, c:c + V7X_LANES]
        sq = blk * blk
        ms_lo = jnp.sum(jnp.where(low, sq, 0.0), axis=-1, keepdims=True) * (1.0 / HEAD_DIM)
        ms_hi = jnp.sum(jnp.where(low, 0.0, sq), axis=-1, keepdims=True) * (1.0 / HEAD_DIM)
        inv = jnp.where(low, lax.rsqrt(ms_lo + EPS), lax.rsqrt(ms_hi + EPS))
        outs.append(blk * inv * g[:, c:c + V7X_LANES])
    return jnp.concatenate(outs, axis=-1)


def _log_sigmoid(x):
    z = -x
    return -(jnp.maximum(z, 0.0) + jnp.log1p(jnp.exp(-jnp.abs(z))))


def _cumsum_lanes(a):
    r, n = a.shape
    lane = lax.broadcasted_iota(jnp.int32, (r, V7X_LANES), 1)
    blocks = []
    offset = jnp.zeros((r, 1), F32)
    for c in range(0, n, V7X_LANES):
        blk = a[:, c:c + V7X_LANES]
        shift = 1
        while shift < V7X_LANES:
            blk = blk + jnp.where(lane >= shift, pltpu.roll(blk, shift, axis=1), 0.0)
            shift *= 2
        blk = blk + offset
        offset = blk[:, V7X_LANES - 1:V7X_LANES]
        blocks.append(blk)
    return jnp.concatenate(blocks, axis=-1)


def _gate_pieces(g, n_heads):
    lane = lax.broadcasted_iota(jnp.int32, g.shape, 1)
    hi = g.astype(BF16).astype(F32)
    rest = g - hi
    mid = rest.astype(BF16).astype(F32)
    low = rest - mid
    pieces = jnp.where(lane < n_heads, hi,
                       jnp.where(lane < 2 * n_heads, pltpu.roll(mid, n_heads, axis=1),
                                 jnp.where(lane < 3 * n_heads, pltpu.roll(low, 2 * n_heads, axis=1), 0.0)))
    return pieces.astype(BF16)


def _pre_mixer_body(x_ref, g1_ref, wgu_ref, wdown_ref, gm_ref, win_ref, wfg_ref, bf_ref, convw_ref,
                    gconv_ref, uinit_ref, ginit_ref,
                    h1_ref, q_ref, kaug_ref, vt_ref, yconv_ref, utail_ref, gtail_ref,
                    act_scr, u_scr, gcarry_scr, *, tiles_per_seq, tail_start, n_heads):
    tm = x_ref.shape[0]
    cw = q_ref.shape[1]
    halo = V7X_SUBLANES

    @pl.when(pl.program_id(0) % tiles_per_seq == 0)
    def _():
        u_scr[0:halo, :] = uinit_ref[...]
        gcarry_scr[...] = ginit_ref[...]

    x = x_ref[...]
    n1 = _rms_norm(x, g1_ref[...]).astype(BF16)
    h1 = x + 0.5 * _swiglu_ffn(n1, wgu_ref, wdown_ref, act_scr)
    h1_ref[...] = h1
    n2 = _rms_norm(h1, gm_ref[...]).astype(BF16)

    def proj(j):
        return _mxu_dot(n2, win_ref[:, j * cw:(j + 1) * cw])

    u = proj(1) * proj(2)
    u_scr[halo:halo + tm, :] = u
    conv = (convw_ref[0:1, :] * u_scr[pl.ds(halo - 2, tm), :]
            + convw_ref[1:2, :] * u_scr[pl.ds(halo - 1, tm), :]
            + convw_ref[2:3, :] * u)
    yconv_ref[...] = _group_norm_lanes(proj(0) * conv, gconv_ref[...]).astype(BF16)
    utail_ref[...] = u[tail_start:tail_start + halo, :]
    u_scr[0:halo, :] = u[tm - halo:tm, :]

    q_ref[...] = (proj(3) * (HEAD_DIM ** -0.5 * LOG2_E)).astype(BF16)
    vt = proj(5).T
    tk = vt_ref.shape[2]
    for c in range(vt_ref.shape[0]):
        vt_ref[c] = vt[:, c * tk:(c + 1) * tk].astype(BF16)

    fg = _mxu_dot(n2, wfg_ref[...]) + bf_ref[...]
    g_run = _cumsum_lanes(_log_sigmoid(fg).T) + gcarry_scr[:, 0:1]
    gcarry_scr[...] = jnp.broadcast_to(g_run[:, tm - 1:tm], gcarry_scr.shape)
    tail_col = tail_start + halo - 1
    gtail_ref[...] = jnp.broadcast_to(g_run[:, tail_col:tail_col + 1], gtail_ref.shape)

    k = proj(4).astype(BF16)
    gate = _gate_pieces((g_run * LOG2_E).T, n_heads)
    for pair in range(cw // HEAD_PAIR):
        base = pair * 2 * HEAD_PAIR
        kaug_ref[:, base:base + HEAD_PAIR] = k[:, pair * HEAD_PAIR:(pair + 1) * HEAD_PAIR]
        kaug_ref[:, base + HEAD_PAIR:base + 2 * HEAD_PAIR] = gate


def _pre_mixer_call(x2, g1, wgu, wdown, gm, win, wfg, bfp, convw, gconv, uinit, ginit, *, tm, tiles_per_seq,
                    tail_start, tk, n_heads):
    n_rows, d = x2.shape
    cw = gconv.shape[1]
    d_ff = wdown.shape[0]
    n_tiles = n_rows // tm
    halo = V7X_SUBLANES

    def resident(shape):
        return pl.BlockSpec(shape, lambda t: (0,) * len(shape), pipeline_mode=pl.Buffered(1))

    def rows(width):
        return pl.BlockSpec((tm, width), lambda t: (t, 0))

    body = functools.partial(_pre_mixer_body, tiles_per_seq=tiles_per_seq, tail_start=tail_start,
                             n_heads=n_heads)
    return pl.pallas_call(
        body,
        grid=(n_tiles,),
        in_specs=[rows(d), resident(g1.shape), resident(wgu.shape), resident(wdown.shape),
                  resident(gm.shape), resident(win.shape), resident(wfg.shape), resident(bfp.shape),
                  resident(convw.shape), resident(gconv.shape), resident(uinit.shape),
                  resident(ginit.shape)],
        out_specs=[rows(d), rows(cw), rows(2 * cw),
                   pl.BlockSpec((tm // tk, cw, tk), lambda t: (t, 0, 0)),
                   rows(cw),
                   pl.BlockSpec((halo, cw), lambda t: (t, 0)),
                   pl.BlockSpec((V7X_LANES, V7X_LANES), lambda t: (t, 0))],
        out_shape=[jax.ShapeDtypeStruct((n_rows, d), F32),
                   jax.ShapeDtypeStruct((n_rows, cw), BF16),
                   jax.ShapeDtypeStruct((n_rows, 2 * cw), BF16),
                   jax.ShapeDtypeStruct((n_rows // tk, cw, tk), BF16),
                   jax.ShapeDtypeStruct((n_rows, cw), BF16),
                   jax.ShapeDtypeStruct((n_tiles * halo, cw), F32),
                   jax.ShapeDtypeStruct((n_tiles * V7X_LANES, V7X_LANES), F32)],
        scratch_shapes=[pltpu.VMEM((tm, d_ff), BF16),
                        pltpu.VMEM((tm + halo, cw), F32),
                        pltpu.VMEM((V7X_LANES, V7X_LANES), F32)],
        compiler_params=pltpu.CompilerParams(dimension_semantics=("arbitrary",),
                                             vmem_limit_bytes=_vmem_limit()),
        name="pre_mixer",
    )(x2, g1, wgu, wdown, gm, win, wfg, bfp, convw, gconv, uinit, ginit)


def _attention_body(q_ref, kaug_ref, vt_ref, kmeta_ref, vtmeta_ref, gattn_ref, y_ref,
                    qm_scr, m_scr, l_scr, acc_scr):
    tq = q_ref.shape[0]
    tk = vt_ref.shape[2]
    n_heads = q_ref.shape[1] // HEAD_DIM
    i = pl.program_id(1)
    neg_inf = float("-inf")

    qt = q_ref[...].astype(F32).T.astype(BF16)
    pair_row = lax.broadcasted_iota(jnp.int32, (HEAD_PAIR, tq), 0)
    for h in range(n_heads):
        pair, half = divmod(h, 2)
        mine = (pair_row >= half * HEAD_DIM) & (pair_row < (half + 1) * HEAD_DIM)
        picks = (pair_row == h) | (pair_row == n_heads + h) | (pair_row == 2 * n_heads + h)
        qm_scr[h, 0:HEAD_PAIR, :] = jnp.where(mine, qt[pair * HEAD_PAIR:(pair + 1) * HEAD_PAIR, :],
                                               jnp.zeros((), BF16))
        qm_scr[h, HEAD_PAIR:2 * HEAD_PAIR, :] = jnp.where(picks, -1.0, 0.0).astype(BF16)

    def chunk_update(kaug_rows, vt_blk, mask, first):
        width = 2 * HEAD_PAIR
        scores = [_mxu_dot(kaug_rows[:, (h // 2) * width:(h // 2 + 1) * width], qm_scr[h])
                  for h in range(n_heads)]
        probs, alphas = [], []
        for h in range(n_heads):
            s = scores[h] if mask is None else jnp.where(mask, scores[h], neg_inf)
            c_max = jnp.max(s, axis=0, keepdims=True)
            if first:
                m_new = c_max
            else:
                m_old = m_scr[h:h + 1, :]
                m_new = jnp.maximum(m_old, c_max)
                alphas.append(jnp.exp2(m_old - m_new))
            p = jnp.exp2(s - m_new)
            p_sum = jnp.sum(p, axis=0, keepdims=True)
            p = p.astype(BF16)
            if vt_blk.shape[1] > p.shape[0]:
                p = jnp.concatenate([p, jnp.zeros((vt_blk.shape[1] - p.shape[0], tq), BF16)], axis=0)
            probs.append(p)
            m_scr[h:h + 1, :] = m_new
            l_scr[h:h + 1, :] = p_sum if first else alphas[h] * l_scr[h:h + 1, :] + p_sum
        for h in range(n_heads):
            head_rows = slice(h * HEAD_DIM, (h + 1) * HEAD_DIM)
            pv = _mxu_dot(vt_blk[head_rows, :], probs[h])
            acc_scr[head_rows, :] = pv if first else alphas[h] * acc_scr[head_rows, :] + pv

    chunk_update(kmeta_ref.at[0:N_META, :], vtmeta_ref, None, first=True)

    def full_chunk(j, carry):
        off = pl.multiple_of(j * tk, tk)
        chunk_update(kaug_ref.at[pl.ds(off, tk), :], vt_ref.at[j], None, first=False)
        return carry

    lax.fori_loop(0, i, full_chunk, 0)

    causal = (lax.broadcasted_iota(jnp.int32, (tk, tq), 0)
              <= lax.broadcasted_iota(jnp.int32, (tk, tq), 1))
    off = pl.multiple_of(i * tk, tk)
    chunk_update(kaug_ref.at[pl.ds(off, tk), :], vt_ref.at[i], causal, first=False)

    for h in range(n_heads):
        head_rows = slice(h * HEAD_DIM, (h + 1) * HEAD_DIM)
        o = acc_scr[head_rows, :] / l_scr[h:h + 1, :]
        acc_scr[head_rows, :] = o * lax.rsqrt(jnp.mean(o * o, axis=0, keepdims=True) + EPS)
    y_ref[...] = (acc_scr[...].T * gattn_ref[...]).astype(BF16)


def _attention_call(q, kaug, vt, kaug_meta, vt_meta, gattn, *, batch, seq, tq):
    cw = q.shape[1]
    tk = vt.shape[2]
    assert tq == tk, "the diagonal chunk assumes query tile == key chunk"
    chunks = seq // tk
    n_heads = cw // HEAD_DIM
    vt4 = vt.reshape(batch, chunks, cw, tk)
    q_tiles = seq // tq

    def const(shape):
        return pl.BlockSpec(shape, lambda b, i: (0,) * len(shape))

    return pl.pallas_call(
        _attention_body,
        grid=(batch, q_tiles),
        in_specs=[pl.BlockSpec((tq, cw), lambda b, i: (b * q_tiles + i, 0)),
                  pl.BlockSpec((seq, 2 * cw), lambda b, i: (b, 0)),
                  pl.BlockSpec((None, chunks, cw, tk), lambda b, i: (b, 0, 0, 0)),
                  const(kaug_meta.shape), const(vt_meta.shape), const(gattn.shape)],
        out_specs=pl.BlockSpec((tq, cw), lambda b, i: (b * q_tiles + i, 0)),
        out_shape=jax.ShapeDtypeStruct(q.shape, BF16),
        scratch_shapes=[pltpu.VMEM((n_heads, 2 * HEAD_PAIR, tq), BF16),
                        pltpu.VMEM((n_heads, tq), F32),
                        pltpu.VMEM((n_heads, tq), F32),
                        pltpu.VMEM((cw, tq), F32)],
        compiler_params=pltpu.CompilerParams(dimension_semantics=("parallel", "parallel"),
                                             vmem_limit_bytes=_vmem_limit()),
        name="fox_attention",
    )(q, kaug, vt4, kaug_meta, vt_meta, gattn)


def _post_mixer_body(h1_ref, yc_ref, ya_ref, wout_ref, g2_ref, wgu_ref, wdown_ref, gf_ref, out_ref, act_scr):
    cw = yc_ref.shape[1]
    h2 = (h1_ref[...]
          + _mxu_dot(yc_ref[...], wout_ref[0:cw, :])
          + _mxu_dot(ya_ref[...], wout_ref[cw:2 * cw, :]))
    n = _rms_norm(h2, g2_ref[...]).astype(BF16)
    h3 = h2 + 0.5 * _swiglu_ffn(n, wgu_ref, wdown_ref, act_scr)
    out_ref[...] = _rms_norm(h3, gf_ref[...])


def _post_mixer_call(h1, yconv, yattn, wout, g2, wgu, wdown, gf, *, tm):
    n_rows, d = h1.shape
    cw = yconv.shape[1]
    d_ff = wdown.shape[0]

    def resident(shape):
        return pl.BlockSpec(shape, lambda t: (0,) * len(shape), pipeline_mode=pl.Buffered(1))

    def rows(width):
        return pl.BlockSpec((tm, width), lambda t: (t, 0))

    return pl.pallas_call(
        _post_mixer_body,
        grid=(n_rows // tm,),
        in_specs=[rows(d), rows(cw), rows(cw), resident(wout.shape), resident(g2.shape),
                  resident(wgu.shape), resident(wdown.shape), resident(gf.shape)],
        out_specs=rows(d),
        out_shape=jax.ShapeDtypeStruct((n_rows, d), F32),
        scratch_shapes=[pltpu.VMEM((tm, d_ff), BF16)],
        compiler_params=pltpu.CompilerParams(dimension_semantics=("parallel",),
                                             vmem_limit_bytes=_vmem_limit()),
        name="post_mixer",
    )(h1, yconv, yattn, wout, g2, wgu, wdown, gf)


def _vmem_limit():
    return V7X_VMEM_BYTES * 7 // 8


def _layer(x, meta_tokens, ffn1_norm, ffn1_w_gu, ffn1_w_down, mix_norm, w_in, conv_w, b_f, out_norm_conv,
           out_norm_attn, w_out, ffn2_norm, ffn2_w_gu, ffn2_w_down, final_norm, *, tm, tq):
    batch, seq, d = x.shape
    assert ffn1_norm.shape[0] == 1, "one layer deep"
    assert seq % tm == 0 and tm % tq == 0
    cw = conv_w.shape[2]
    n_heads = b_f.shape[1]
    row = lambda a: a.reshape(1, -1).astype(F32)

    wgu1, wdown1 = ffn1_w_gu[0].astype(BF16), ffn1_w_down[0].astype(BF16)
    wgu2, wdown2 = ffn2_w_gu[0].astype(BF16), ffn2_w_down[0].astype(BF16)
    win = w_in[0, :, :6 * cw].astype(BF16)
    wfg = jnp.pad(w_in[0, :, 6 * cw:], ((0, 0), (0, V7X_LANES - n_heads))).astype(BF16)
    bfp = jnp.pad(row(b_f[0]), ((0, 0), (0, V7X_LANES - n_heads)))
    wout = w_out[0].astype(BF16)
    pre_args = (row(ffn1_norm[0]), wgu1, wdown1, row(mix_norm[0]), win, wfg, bfp, conv_w[0].astype(F32),
                row(out_norm_conv[0]))

    meta_rows = V7X_LANES
    x_meta = jnp.pad(meta_tokens.astype(F32), ((0, meta_rows - N_META), (0, 0)))
    _, _, kaug_meta, vt_meta, _, u_meta, g_meta = _pre_mixer_call(
        x_meta, *pre_args, jnp.zeros((V7X_SUBLANES, cw), F32), jnp.zeros((V7X_LANES, V7X_LANES), F32),
        tm=meta_rows, tiles_per_seq=1, tail_start=N_META - V7X_SUBLANES, tk=meta_rows, n_heads=n_heads)

    h1, q, kaug, vt, yconv, _, _ = _pre_mixer_call(
        x.reshape(batch * seq, d), *pre_args, u_meta, g_meta,
        tm=tm, tiles_per_seq=seq // tm, tail_start=tm - V7X_SUBLANES, tk=tq, n_heads=n_heads)

    yattn = _attention_call(q, kaug, vt, kaug_meta, vt_meta.reshape(cw, meta_rows),
                            row(out_norm_attn[0]), batch=batch, seq=seq, tq=tq)

    out = _post_mixer_call(h1, yconv, yattn, wout, row(ffn2_norm[0]), wgu2, wdown2, row(final_norm), tm=tm)
    return out.reshape(batch, seq, d)


def kernel(x, meta_tokens, ffn1_norm, ffn1_w_gu, ffn1_w_down, mix_norm, w_in, conv_w, b_f, out_norm_conv,
           out_norm_attn, w_out, ffn2_norm, ffn2_w_gu, ffn2_w_down, final_norm):
    return _layer(x, meta_tokens, ffn1_norm, ffn1_w_gu, ffn1_w_down, mix_norm, w_in, conv_w, b_f,
                  out_norm_conv, out_norm_attn, w_out, ffn2_norm, ffn2_w_gu, ffn2_w_down, final_norm,
                  tm=TOKEN_TILE, tq=ATTN_TILE)
```

```python
import functools
import math

import jax
import jax.numpy as jnp
from jax import lax
from jax.experimental import pallas as pl
from jax.experimental.pallas import tpu as pltpu

EPS = 1e-6
LOG2_E = math.log2(math.e)
N_META = 16
HEAD_DIM = 64
CONV_K = 3

V7X_LANES = 128
V7X_SUBLANES = 8
V7X_MXU_DIM = 256
V7X_VMEM_BYTES = 64 * 1024 * 1024

TOKEN_TILE = 512
ATTN_KEY_CHUNK = V7X_MXU_DIM
ATTN_TILE = 2 * ATTN_KEY_CHUNK
FF_CHUNK = V7X_MXU_DIM
HEAD_PAIR = 2 * HEAD_DIM

F32 = jnp.float32
BF16 = jnp.bfloat16


def _mxu_dot(a, b):
    return jnp.dot(a, b, preferred_element_type=F32)


def _rms_norm(x, g):
    return x * lax.rsqrt(jnp.mean(x * x, axis=-1, keepdims=True) + EPS) * g


def _swiglu_ffn(n, wgu_ref, wdown_ref, act_scr):
    d_ff = wdown_ref.shape[0]
    for lo in range(0, d_ff, FF_CHUNK):
        g = _mxu_dot(n, wgu_ref[:, lo:lo + FF_CHUNK])
        u = _mxu_dot(n, wgu_ref[:, d_ff + lo:d_ff + lo + FF_CHUNK])
        act_scr[:, lo:lo + FF_CHUNK] = (g * jax.nn.sigmoid(g) * u).astype(BF16)
    return _mxu_dot(act_scr[...], wdown_ref[...])


def _group_norm_lanes(y, g):
    lane = lax.broadcasted_iota(jnp.int32, (y.shape[0], V7X_LANES), 1)
    low = lane < HEAD_DIM
    outs = []
    for c in range(0, y.shape[1], V7X_LANES):
        blk = y[:, c:c + V7X_LANES]
        sq = blk * blk
        ms_lo = jnp.sum(jnp.where(low, sq, 0.0), axis=-1, keepdims=True) * (1.0 / HEAD_DIM)
        ms_hi = jnp.sum(jnp.where(low, 0.0, sq), axis=-1, keepdims=True) * (1.0 / HEAD_DIM)
        inv = jnp.where(low, lax.rsqrt(ms_lo + EPS), lax.rsqrt(ms_hi + EPS))
        outs.append(blk * inv * g[:, c:c + V7X_LANES])
    return jnp.concatenate(outs, axis=-1)


def _log_sigmoid(x):
    z = -x
    return -(jnp.maximum(z, 0.0) + jnp.log1p(jnp.exp(-jnp.abs(z))))


def _cumsum_lanes(a):
    r, n = a.shape
    lane = lax.broadcasted_iota(jnp.int32, (r, V7X_LANES), 1)
    blocks = []
    offset = jnp.zeros((r, 1), F32)
    for c in range(0, n, V7X_LANES):
        blk = a[:, c:c + V7X_LANES]
        shift = 1
        while shift < V7X_LANES:
            blk = blk + jnp.where(lane >= shift, pltpu.roll(blk, shift, axis=1), 0.0)
            shift *= 2
        blk = blk + offset
        offset = blk[:, V7X_LANES - 1:V7X_LANES]
        blocks.append(blk)
    return jnp.concatenate(blocks, axis=-1)


def _gate_pieces(g, n_heads):
    lane = lax.broadcasted_iota(jnp.int32, g.shape, 1)
    hi = g.astype(BF16).astype(F32)
    rest = g - hi
    mid = rest.astype(BF16).astype(F32)
    low = rest - mid
    pieces = jnp.where(lane < n_heads, hi,
                       jnp.where(lane < 2 * n_heads, pltpu.roll(mid, n_heads, axis=1),
                                 jnp.where(lane < 3 * n_heads, pltpu.roll(low, 2 * n_heads, axis=1), 0.0)))
    return pieces.astype(BF16)


def _pre_mixer_body(x_ref, g1_ref, wgu_ref, wdown_ref, gm_ref, win_ref, wfg_ref, bf_ref, convw_ref,
                    gconv_ref, uinit_ref, ginit_ref,
                    h1_ref, q_ref, kaug_ref, vt_ref, yconv_ref, utail_ref, gtail_ref,
                    act_scr, u_scr, gcarry_scr, *, tiles_per_seq, tail_start, n_heads):
    tm = x_ref.shape[0]
    cw = q_ref.shape[1]
    halo = V7X_SUBLANES

    @pl.when(pl.program_id(0) % tiles_per_seq == 0)
    def _():
        u_scr[0:halo, :] = uinit_ref[...]
        gcarry_scr[...] = ginit_ref[...]

    x = x_ref[...]
    n1 = _rms_norm(x, g1_ref[...]).astype(BF16)
    h1 = x + 0.5 * _swiglu_ffn(n1, wgu_ref, wdown_ref, act_scr)
    h1_ref[...] = h1
    n2 = _rms_norm(h1, gm_ref[...]).astype(BF16)

    def proj(j):
        return _mxu_dot(n2, win_ref[:, j * cw:(j + 1) * cw])

    u = proj(1) * proj(2)
    u_scr[halo:halo + tm, :] = u
    conv = (convw_ref[0:1, :] * u_scr[pl.ds(halo - 2, tm), :]
            + convw_ref[1:2, :] * u_scr[pl.ds(halo - 1, tm), :]
            + convw_ref[2:3, :] * u)
    yconv_ref[...] = _group_norm_lanes(proj(0) * conv, gconv_ref[...]).astype(BF16)
    utail_ref[...] = u[tail_start:tail_start + halo, :]
    u_scr[0:halo, :] = u[tm - halo:tm, :]

    q_ref[...] = (proj(3) * (HEAD_DIM ** -0.5 * LOG2_E)).astype(BF16)
    vt = proj(5).T
    tk = vt_ref.shape[2]
    for c in range(vt_ref.shape[0]):
        vt_ref[c] = vt[:, c * tk:(c + 1) * tk].astype(BF16)

    fg = _mxu_dot(n2, wfg_ref[...]) + bf_ref[...]
    g_run = _cumsum_lanes(_log_sigmoid(fg).T) + gcarry_scr[:, 0:1]
    gcarry_scr[...] = jnp.broadcast_to(g_run[:, tm - 1:tm], gcarry_scr.shape)
    tail_col = tail_start + halo - 1
    gtail_ref[...] = jnp.broadcast_to(g_run[:, tail_col:tail_col + 1], gtail_ref.shape)

    k = proj(4).astype(BF16)
    gate = _gate_pieces((g_run * LOG2_E).T, n_heads)
    for pair in range(cw // HEAD_PAIR):
        base = pair * 2 * HEAD_PAIR
        kaug_ref[:, base:base + HEAD_PAIR] = k[:, pair * HEAD_PAIR:(pair + 1) * HEAD_PAIR]
        kaug_ref[:, base + HEAD_PAIR:base + 2 * HEAD_PAIR] = gate


def _pre_mixer_call(x2, g1, wgu, wdown, gm, win, wfg, bfp, convw, gconv, uinit, ginit, *, tm, tiles_per_seq,
                    tail_start, tk, n_heads):
    n_rows, d = x2.shape
    cw = gconv.shape[1]
    d_ff = wdown.shape[0]
    n_tiles = n_rows // tm
    halo = V7X_SUBLANES

    def resident(shape):
        return pl.BlockSpec(shape, lambda t: (0,) * len(shape), pipeline_mode=pl.Buffered(1))

    def rows(width):
        return pl.BlockSpec((tm, width), lambda t: (t, 0))

    body = functools.partial(_pre_mixer_body, tiles_per_seq=tiles_per_seq, tail_start=tail_start,
                             n_heads=n_heads)
    return pl.pallas_call(
        body,
        grid=(n_tiles,),
        in_specs=[rows(d), resident(g1.shape), resident(wgu.shape), resident(wdown.shape),
                  resident(gm.shape), resident(win.shape), resident(wfg.shape), resident(bfp.shape),
                  resident(convw.shape), resident(gconv.shape), resident(uinit.shape),
                  resident(ginit.shape)],
        out_specs=[rows(d), rows(cw), rows(2 * cw),
                   pl.BlockSpec((tm // tk, cw, tk), lambda t: (t, 0, 0)),
                   rows(cw),
                   pl.BlockSpec((halo, cw), lambda t: (t, 0)),
                   pl.BlockSpec((V7X_LANES, V7X_LANES), lambda t: (t, 0))],
        out_shape=[jax.ShapeDtypeStruct((n_rows, d), F32),
                   jax.ShapeDtypeStruct((n_rows, cw), BF16),
                   jax.ShapeDtypeStruct((n_rows, 2 * cw), BF16),
                   jax.ShapeDtypeStruct((n_rows // tk, cw, tk), BF16),
                   jax.ShapeDtypeStruct((n_rows, cw), BF16),
                   jax.ShapeDtypeStruct((n_tiles * halo, cw), F32),
                   jax.ShapeDtypeStruct((n_tiles * V7X_LANES, V7X_LANES), F32)],
        scratch_shapes=[pltpu.VMEM((tm, d_ff), BF16),
                        pltpu.VMEM((tm + halo, cw), F32),
                        pltpu.VMEM((V7X_LANES, V7X_LANES), F32)],
        compiler_params=pltpu.CompilerParams(dimension_semantics=("arbitrary",),
                                             vmem_limit_bytes=_vmem_limit()),
        name="pre_mixer",
    )(x2, g1, wgu, wdown, gm, win, wfg, bfp, convw, gconv, uinit, ginit)


def _attention_body(q_ref, kaug_ref, vt_ref, kmeta_ref, vtmeta_ref, gattn_ref, y_ref,
                    qm_scr, s_scr, m_scr, l_scr, acc_scr):
    tq = q_ref.shape[0]
    tk = vt_ref.shape[2]
    n_heads = q_ref.shape[1] // HEAD_DIM
    i = pl.program_id(1)
    neg_inf = float("-inf")

    qt = q_ref[...].astype(F32).T.astype(BF16)
    pair_row = lax.broadcasted_iota(jnp.int32, (HEAD_PAIR, tq), 0)
    for h in range(n_heads):
        pair, half = divmod(h, 2)
        mine = (pair_row >= half * HEAD_DIM) & (pair_row < (half + 1) * HEAD_DIM)
        picks = (pair_row == h) | (pair_row == n_heads + h) | (pair_row == 2 * n_heads + h)
        qm_scr[h, 0:HEAD_PAIR, :] = jnp.where(mine, qt[pair * HEAD_PAIR:(pair + 1) * HEAD_PAIR, :],
                                               jnp.zeros((), BF16))
        qm_scr[h, HEAD_PAIR:2 * HEAD_PAIR, :] = jnp.where(picks, -1.0, 0.0).astype(BF16)

    all_q = slice(0, tq)
    late_q = slice(tq - tk, tq)

    def qk(kaug_rows, h, lanes):
        width = 2 * HEAD_PAIR
        return _mxu_dot(kaug_rows[:, (h // 2) * width:(h // 2 + 1) * width], qm_scr[h, :, lanes])

    def key_rows(c):
        return kaug_ref.at[pl.ds(pl.multiple_of(c * tk, tk), tk), :]

    def stage_scores(c, slot):
        rows = key_rows(c)
        for h in range(n_heads):
            s_scr[slot, h] = qk(rows, h, all_q)

    def fold(scores_of, vt_blk, mask, first, lanes):
        n = lanes.stop - lanes.start
        probs, alphas = [], []
        for h in range(n_heads):
            s = scores_of(h)
            if mask is not None:
                s = jnp.where(mask, s, neg_inf)
            c_max = jnp.max(s, axis=0, keepdims=True)
            if first:
                m_new = c_max
            else:
                m_old = m_scr[h:h + 1, lanes]
                m_new = jnp.maximum(m_old, c_max)
                alphas.append(jnp.exp2(m_old - m_new))
            p = jnp.exp2(s - m_new)
            p_sum = jnp.sum(p, axis=0, keepdims=True)
            p = p.astype(BF16)
            if vt_blk.shape[1] > p.shape[0]:
                p = jnp.concatenate([p, jnp.zeros((vt_blk.shape[1] - p.shape[0], n), BF16)], axis=0)
            probs.append(p)
            m_scr[h:h + 1, lanes] = m_new
            l_scr[h:h + 1, lanes] = p_sum if first else alphas[h] * l_scr[h:h + 1, lanes] + p_sum
        for h in range(n_heads):
            head_rows = slice(h * HEAD_DIM, (h + 1) * HEAD_DIM)
            pv = _mxu_dot(vt_blk[head_rows, :], probs[h])
            acc_scr[head_rows, lanes] = pv if first else alphas[h] * acc_scr[head_rows, lanes] + pv

    meta_rows = kmeta_ref.at[0:N_META, :]
    fold(lambda h: qk(meta_rows, h, all_q), vtmeta_ref, None, True, all_q)

    stage_scores(0, 0)

    def two_chunks(jj, carry):
        c = 2 * jj
        stage_scores(c + 1, 1)
        fold(lambda h: s_scr[0, h], vt_ref.at[c], None, False, all_q)
        stage_scores(c + 2, 0)
        fold(lambda h: s_scr[1, h], vt_ref.at[c + 1], None, False, all_q)
        return carry

    lax.fori_loop(0, i, two_chunks, 0)

    c = 2 * i
    last_rows = key_rows(c + 1)
    late_scores = [qk(last_rows, h, late_q) for h in range(n_heads)]
    causal_all = (lax.broadcasted_iota(jnp.int32, (tk, tq), 0)
                  <= lax.broadcasted_iota(jnp.int32, (tk, tq), 1))
    fold(lambda h: s_scr[0, h], vt_ref.at[c], causal_all, False, all_q)
    fold(lambda h: late_scores[h], vt_ref.at[c + 1], causal_all[:, 0:tk], False, late_q)

    for h in range(n_heads):
        head_rows = slice(h * HEAD_DIM, (h + 1) * HEAD_DIM)
        o = acc_scr[head_rows, :] / l_scr[h:h + 1, :]
        acc_scr[head_rows, :] = o * lax.rsqrt(jnp.mean(o * o, axis=0, keepdims=True) + EPS)
    y_ref[...] = (acc_scr[...].T * gattn_ref[...]).astype(BF16)


def _attention_call(q, kaug, vt, kaug_meta, vt_meta, gattn, *, batch, seq, tq):
    cw = q.shape[1]
    tk = vt.shape[2]
    assert tq == 2 * tk, "the diagonal handling assumes two key chunks per query tile"
    chunks = seq // tk
    n_heads = cw // HEAD_DIM
    vt4 = vt.reshape(batch, chunks, cw, tk)
    q_tiles = seq // tq

    def const(shape):
        return pl.BlockSpec(shape, lambda b, i: (0,) * len(shape))

    return pl.pallas_call(
        _attention_body,
        grid=(batch, q_tiles),
        in_specs=[pl.BlockSpec((tq, cw), lambda b, i: (b * q_tiles + i, 0)),
                  pl.BlockSpec((seq, 2 * cw), lambda b, i: (b, 0)),
                  pl.BlockSpec((None, chunks, cw, tk), lambda b, i: (b, 0, 0, 0)),
                  const(kaug_meta.shape), const(vt_meta.shape), const(gattn.shape)],
        out_specs=pl.BlockSpec((tq, cw), lambda b, i: (b * q_tiles + i, 0)),
        out_shape=jax.ShapeDtypeStruct(q.shape, BF16),
        scratch_shapes=[pltpu.VMEM((n_heads, 2 * HEAD_PAIR, tq), BF16),
                        pltpu.VMEM((2, n_heads, tk, tq), F32),
                        pltpu.VMEM((n_heads, tq), F32),
                        pltpu.VMEM((n_heads, tq), F32),
                        pltpu.VMEM((cw, tq), F32)],
        compiler_params=pltpu.CompilerParams(dimension_semantics=("parallel", "parallel"),
                                             vmem_limit_bytes=_vmem_limit()),
        name="fox_attention",
    )(q, kaug, vt4, kaug_meta, vt_meta, gattn)


def _post_mixer_body(h1_ref, yc_ref, ya_ref, wout_ref, g2_ref, wgu_ref, wdown_ref, gf_ref, out_ref, act_scr):
    cw = yc_ref.shape[1]
    h2 = (h1_ref[...]
          + _mxu_dot(yc_ref[...], wout_ref[0:cw, :])
          + _mxu_dot(ya_ref[...], wout_ref[cw:2 * cw, :]))
    n = _rms_norm(h2, g2_ref[...]).astype(BF16)
    h3 = h2 + 0.5 * _swiglu_ffn(n, wgu_ref, wdown_ref, act_scr)
    out_ref[...] = _rms_norm(h3, gf_ref[...])


def _post_mixer_call(h1, yconv, yattn, wout, g2, wgu, wdown, gf, *, tm):
    n_rows, d = h1.shape
    cw = yconv.shape[1]
    d_ff = wdown.shape[0]

    def resident(shape):
        return pl.BlockSpec(shape, lambda t: (0,) * len(shape), pipeline_mode=pl.Buffered(1))

    def rows(width):
        return pl.BlockSpec((tm, width), lambda t: (t, 0))

    return pl.pallas_call(
        _post_mixer_body,
        grid=(n_rows // tm,),
        in_specs=[rows(d), rows(cw), rows(cw), resident(wout.shape), resident(g2.shape),
                  resident(wgu.shape), resident(wdown.shape), resident(gf.shape)],
        out_specs=rows(d),
        out_shape=jax.ShapeDtypeStruct((n_rows, d), F32),
        scratch_shapes=[pltpu.VMEM((tm, d_ff), BF16)],
        compiler_params=pltpu.CompilerParams(dimension_semantics=("parallel",),
                                             vmem_limit_bytes=_vmem_limit()),
        name="post_mixer",
    )(h1, yconv, yattn, wout, g2, wgu, wdown, gf)


def _vmem_limit():
    return V7X_VMEM_BYTES * 7 // 8


def _layer(x, meta_tokens, ffn1_norm, ffn1_w_gu, ffn1_w_down, mix_norm, w_in, conv_w, b_f, out_norm_conv,
           out_norm_attn, w_out, ffn2_norm, ffn2_w_gu, ffn2_w_down, final_norm, *, tm, tq):
    batch, seq, d = x.shape
    assert ffn1_norm.shape[0] == 1, "one layer deep"
    assert seq % tm == 0 and seq % tq == 0 and tm % ATTN_KEY_CHUNK == 0
    cw = conv_w.shape[2]
    n_heads = b_f.shape[1]
    row = lambda a: a.reshape(1, -1).astype(F32)

    wgu1, wdown1 = ffn1_w_gu[0].astype(BF16), ffn1_w_down[0].astype(BF16)
    wgu2, wdown2 = ffn2_w_gu[0].astype(BF16), ffn2_w_down[0].astype(BF16)
    win = w_in[0, :, :6 * cw].astype(BF16)
    wfg = jnp.pad(w_in[0, :, 6 * cw:], ((0, 0), (0, V7X_LANES - n_heads))).astype(BF16)
    bfp = jnp.pad(row(b_f[0]), ((0, 0), (0, V7X_LANES - n_heads)))
    wout = w_out[0].astype(BF16)
    pre_args = (row(ffn1_norm[0]), wgu1, wdown1, row(mix_norm[0]), win, wfg, bfp, conv_w[0].astype(F32),
                row(out_norm_conv[0]))

    meta_rows = V7X_LANES
    x_meta = jnp.pad(meta_tokens.astype(F32), ((0, meta_rows - N_META), (0, 0)))
    _, _, kaug_meta, vt_meta, _, u_meta, g_meta = _pre_mixer_call(
        x_meta, *pre_args, jnp.zeros((V7X_SUBLANES, cw), F32), jnp.zeros((V7X_LANES, V7X_LANES), F32),
        tm=meta_rows, tiles_per_seq=1, tail_start=N_META - V7X_SUBLANES, tk=meta_rows, n_heads=n_heads)

    h1, q, kaug, vt, yconv, _, _ = _pre_mixer_call(
        x.reshape(batch * seq, d), *pre_args, u_meta, g_meta,
        tm=tm, tiles_per_seq=seq // tm, tail_start=tm - V7X_SUBLANES, tk=ATTN_KEY_CHUNK, n_heads=n_heads)

    yattn = _attention_call(q, kaug, vt, kaug_meta, vt_meta.reshape(cw, meta_rows),
                            row(out_norm_attn[0]), batch=batch, seq=seq, tq=tq)

    out = _post_mixer_call(h1, yconv, yattn, wout, row(ffn2_norm[0]), wgu2, wdown2, row(final_norm), tm=tm)
    return out.reshape(batch, seq, d)


def kernel(x, meta_tokens, ffn1_norm, ffn1_w_gu, ffn1_w_down, mix_norm, w_in, conv_w, b_f, out_norm_conv,
           out_norm_attn, w_out, ffn2_norm, ffn2_w_gu, ffn2_w_down, final_norm):
    return _layer(x, meta_tokens, ffn1_norm, ffn1_w_gu, ffn1_w_down, mix_norm, w_in, conv_w, b_f,
                  out_norm_conv, out_norm_attn, w_out, ffn2_norm, ffn2_w_gu, ffn2_w_down, final_norm,
                  tm=TOKEN_TILE, tq=ATTN_TILE)
```

```python
import functools
import math

import jax
import jax.numpy as jnp
from jax import lax
from jax.experimental import pallas as pl
from jax.experimental.pallas import tpu as pltpu

EPS = 1e-6
LOG2_E = math.log2(math.e)
N_META = 16
HEAD_DIM = 64
CONV_K = 3

V7X_LANES = 128
V7X_SUBLANES = 8
V7X_MXU_DIM = 256
V7X_VMEM_BYTES = 64 * 1024 * 1024

TOKEN_TILE = 512
ROW_SUBTILE = 256
ATTN_KEY_CHUNK = V7X_MXU_DIM
ATTN_TILE = 2 * ATTN_KEY_CHUNK
FF_CHUNK = V7X_MXU_DIM
HEAD_PAIR = 2 * HEAD_DIM

F32 = jnp.float32
BF16 = jnp.bfloat16


def _mxu_dot(a, b):
    return jnp.dot(a, b, preferred_element_type=F32)


def _rms_norm(x, g):
    return x * lax.rsqrt(jnp.mean(x * x, axis=-1, keepdims=True) + EPS) * g


def _row_parts(tm):
    sub = min(tm, ROW_SUBTILE)
    return [slice(r, r + sub) for r in range(0, tm, sub)]


def _swiglu_ffn(ns, parts, wgu_ref, wdown_ref, act_scr):
    d_ff = wdown_ref.shape[0]
    for n, r in zip(ns, parts):
        for lo in range(0, d_ff, FF_CHUNK):
            g = _mxu_dot(n, wgu_ref[:, lo:lo + FF_CHUNK])
            u = _mxu_dot(n, wgu_ref[:, d_ff + lo:d_ff + lo + FF_CHUNK])
            act_scr[r, lo:lo + FF_CHUNK] = (g * jax.nn.sigmoid(g) * u).astype(BF16)
    return [_mxu_dot(act_scr[r, :], wdown_ref[...]) for r in parts]


def _group_norm_lanes(y, g):
    lane = lax.broadcasted_iota(jnp.int32, (y.shape[0], V7X_LANES), 1)
    low = lane < HEAD_DIM
    outs = []
    for c in range(0, y.shape[1], V7X_LANES):
        blk = y[:, c:c + V7X_LANES]
        sq = blk * blk
        ms_lo = jnp.sum(jnp.where(low, sq, 0.0), axis=-1, keepdims=True) * (1.0 / HEAD_DIM)
        ms_hi = jnp.sum(jnp.where(low, 0.0, sq), axis=-1, keepdims=True) * (1.0 / HEAD_DIM)
        inv = jnp.where(low, lax.rsqrt(ms_lo + EPS), lax.rsqrt(ms_hi + EPS))
        outs.append(blk * inv * g[:, c:c + V7X_LANES])
    return jnp.concatenate(outs, axis=-1)


def _log_sigmoid(x):
    z = -x
    return -(jnp.maximum(z, 0.0) + jnp.log1p(jnp.exp(-jnp.abs(z))))


def _cumsum_lanes(a):
    r, n = a.shape
    lane = lax.broadcasted_iota(jnp.int32, (r, V7X_LANES), 1)
    blocks = []
    offset = jnp.zeros((r, 1), F32)
    for c in range(0, n, V7X_LANES):
        blk = a[:, c:c + V7X_LANES]
        shift = 1
        while shift < V7X_LANES:
            blk = blk + jnp.where(lane >= shift, pltpu.roll(blk, shift, axis=1), 0.0)
            shift *= 2
        blk = blk + offset
        offset = blk[:, V7X_LANES - 1:V7X_LANES]
        blocks.append(blk)
    return jnp.concatenate(blocks, axis=-1)


def _gate_pieces(g, n_heads):
    lane = lax.broadcasted_iota(jnp.int32, g.shape, 1)
    hi = g.astype(BF16).astype(F32)
    rest = g - hi
    mid = rest.astype(BF16).astype(F32)
    low = rest - mid
    pieces = jnp.where(lane < n_heads, hi,
                       jnp.where(lane < 2 * n_heads, pltpu.roll(mid, n_heads, axis=1),
                                 jnp.where(lane < 3 * n_heads, pltpu.roll(low, 2 * n_heads, axis=1), 0.0)))
    return pieces.astype(BF16)


def _pre_mixer_body(x_ref, g1_ref, wgu_ref, wdown_ref, gm_ref, win_ref, wfg_ref, bf_ref, convw_ref,
                    gconv_ref, uinit_ref, ginit_ref,
                    h1_ref, q_ref, kaug_ref, vt_ref, yconv_ref, utail_ref, gtail_ref,
                    act_scr, u_scr, gcarry_scr, *, tiles_per_seq, tail_start, n_heads):
    tm = x_ref.shape[0]
    cw = q_ref.shape[1]
    halo = V7X_SUBLANES

    @pl.when(pl.program_id(0) % tiles_per_seq == 0)
    def _():
        u_scr[0:halo, :] = uinit_ref[...]
        gcarry_scr[...] = ginit_ref[...]

    parts = _row_parts(tm)
    sub = parts[0].stop
    xs = [x_ref[r, :] for r in parts]
    n1 = [_rms_norm(x, g1_ref[...]).astype(BF16) for x in xs]
    ffn = _swiglu_ffn(n1, parts, wgu_ref, wdown_ref, act_scr)
    n2 = []
    for x, f, r in zip(xs, ffn, parts):
        h1 = x + 0.5 * f
        h1_ref[r, :] = h1
        n2.append(_rms_norm(h1, gm_ref[...]).astype(BF16))

    def proj(n, j):
        return _mxu_dot(n, win_ref[:, j * cw:(j + 1) * cw])

    tail_col = tail_start + halo - 1
    g_carry = gcarry_scr[0:n_heads, 0:1]
    gates = []
    for n, r in zip(n2, parts):
        fg = _mxu_dot(n, wfg_ref[...]) + bf_ref[...]
        g_run = _cumsum_lanes(_log_sigmoid(fg).T[0:n_heads, :]) + g_carry
        g_carry = g_run[:, sub - 1:sub]
        if r.start <= tail_col < r.stop:
            gtail_ref[...] = jnp.broadcast_to(g_run[:, tail_col - r.start:tail_col - r.start + 1],
                                              gtail_ref.shape)
        g_rows = jnp.concatenate([g_run * LOG2_E, jnp.zeros((V7X_LANES - n_heads, sub), F32)], axis=0)
        gates.append(_gate_pieces(g_rows.T, n_heads))
    gcarry_scr[...] = jnp.broadcast_to(g_carry, gcarry_scr.shape)

    us = [proj(n, 1) * proj(n, 2) for n in n2]
    for u, r in zip(us, parts):
        u_scr[halo + r.start:halo + r.stop, :] = u
    for n, u, r in zip(n2, us, parts):
        conv = (convw_ref[0:1, :] * u_scr[pl.ds(halo - 2 + r.start, sub), :]
                + convw_ref[1:2, :] * u_scr[pl.ds(halo - 1 + r.start, sub), :]
                + convw_ref[2:3, :] * u)
        yconv_ref[r, :] = _group_norm_lanes(proj(n, 0) * conv, gconv_ref[...]).astype(BF16)
    utail_ref[...] = u_scr[halo + tail_start:halo + tail_start + halo, :]
    u_scr[0:halo, :] = u_scr[tm:tm + halo, :]

    tk = vt_ref.shape[2]
    for n, r in zip(n2, parts):
        q_ref[r, :] = (proj(n, 3) * (HEAD_DIM ** -0.5 * LOG2_E)).astype(BF16)
        vt = proj(n, 5).T
        for c in range(sub // tk):
            vt_ref[r.start // tk + c] = vt[:, c * tk:(c + 1) * tk].astype(BF16)

    for n, gate, r in zip(n2, gates, parts):
        k = proj(n, 4).astype(BF16)
        for pair in range(cw // HEAD_PAIR):
            base = pair * 2 * HEAD_PAIR
            kaug_ref[r, base:base + HEAD_PAIR] = k[:, pair * HEAD_PAIR:(pair + 1) * HEAD_PAIR]
            kaug_ref[r, base + HEAD_PAIR:base + 2 * HEAD_PAIR] = gate


def _pre_mixer_call(x2, g1, wgu, wdown, gm, win, wfg, bfp, convw, gconv, uinit, ginit, *, tm, tiles_per_seq,
                    tail_start, tk, n_heads):
    n_rows, d = x2.shape
    cw = gconv.shape[1]
    d_ff = wdown.shape[0]
    n_tiles = n_rows // tm
    halo = V7X_SUBLANES

    def resident(shape):
        return pl.BlockSpec(shape, lambda t: (0,) * len(shape), pipeline_mode=pl.Buffered(1))

    def rows(width):
        return pl.BlockSpec((tm, width), lambda t: (t, 0))

    body = functools.partial(_pre_mixer_body, tiles_per_seq=tiles_per_seq, tail_start=tail_start,
                             n_heads=n_heads)
    return pl.pallas_call(
        body,
        grid=(n_tiles,),
        in_specs=[rows(d), resident(g1.shape), resident(wgu.shape), resident(wdown.shape),
                  resident(gm.shape), resident(win.shape), resident(wfg.shape), resident(bfp.shape),
                  resident(convw.shape), resident(gconv.shape), resident(uinit.shape),
                  resident(ginit.shape)],
        out_specs=[rows(d), rows(cw), rows(2 * cw),
                   pl.BlockSpec((tm // tk, cw, tk), lambda t: (t, 0, 0)),
                   rows(cw),
                   pl.BlockSpec((halo, cw), lambda t: (t, 0)),
                   pl.BlockSpec((n_heads, V7X_LANES), lambda t: (t, 0))],
        out_shape=[jax.ShapeDtypeStruct((n_rows, d), F32),
                   jax.ShapeDtypeStruct((n_rows, cw), BF16),
                   jax.ShapeDtypeStruct((n_rows, 2 * cw), BF16),
                   jax.ShapeDtypeStruct((n_rows // tk, cw, tk), BF16),
                   jax.ShapeDtypeStruct((n_rows, cw), BF16),
                   jax.ShapeDtypeStruct((n_tiles * halo, cw), F32),
                   jax.ShapeDtypeStruct((n_tiles * n_heads, V7X_LANES), F32)],
        scratch_shapes=[pltpu.VMEM((tm, d_ff), BF16),
                        pltpu.VMEM((tm + halo, cw), F32),
                        pltpu.VMEM((n_heads, V7X_LANES), F32)],
        compiler_params=pltpu.CompilerParams(dimension_semantics=("arbitrary",),
                                             vmem_limit_bytes=_vmem_limit()),
        name="pre_mixer",
    )(x2, g1, wgu, wdown, gm, win, wfg, bfp, convw, gconv, uinit, ginit)


def _attention_body(q_ref, kaug_ref, vt_ref, kmeta_ref, vtmeta_ref, gattn_ref, y_ref,
                    qm_scr, s_scr, m_scr, l_scr, acc_scr):
    tq = q_ref.shape[0]
    tk = vt_ref.shape[2]
    n_heads = q_ref.shape[1] // HEAD_DIM
    i = pl.program_id(1)
    neg_inf = float("-inf")

    qt = q_ref[...].astype(F32).T.astype(BF16)
    pair_row = lax.broadcasted_iota(jnp.int32, (HEAD_PAIR, tq), 0)
    for h in range(n_heads):
        pair, half = divmod(h, 2)
        mine = (pair_row >= half * HEAD_DIM) & (pair_row < (half + 1) * HEAD_DIM)
        picks = (pair_row == h) | (pair_row == n_heads + h) | (pair_row == 2 * n_heads + h)
        qm_scr[h, 0:HEAD_PAIR, :] = jnp.where(mine, qt[pair * HEAD_PAIR:(pair + 1) * HEAD_PAIR, :],
                                               jnp.zeros((), BF16))
        qm_scr[h, HEAD_PAIR:2 * HEAD_PAIR, :] = jnp.where(picks, -1.0, 0.0).astype(BF16)

    all_q = slice(0, tq)
    late_q = slice(tq - tk, tq)

    def qk(kaug_rows, h, lanes):
        width = 2 * HEAD_PAIR
        return _mxu_dot(kaug_rows[:, (h // 2) * width:(h // 2 + 1) * width], qm_scr[h, :, lanes])

    def key_rows(c):
        return kaug_ref.at[pl.ds(pl.multiple_of(c * tk, tk), tk), :]

    def stage_scores(c, slot):
        rows = key_rows(c)
        for h in range(n_heads):
            s_scr[slot, h] = qk(rows, h, all_q)

    def fold(scores_of, vt_blk, mask, first, lanes):
        n = lanes.stop - lanes.start
        probs, alphas = [], []
        for h in range(n_heads):
            s = scores_of(h)
            if mask is not None:
                s = jnp.where(mask, s, neg_inf)
            c_max = jnp.max(s, axis=0, keepdims=True)
            if first:
                m_new = c_max
            else:
                m_old = m_scr[h:h + 1, lanes]
                m_new = jnp.maximum(m_old, c_max)
                alphas.append(jnp.exp2(m_old - m_new))
            p = jnp.exp2(s - m_new)
            p_sum = jnp.sum(p, axis=0, keepdims=True)
            p = p.astype(BF16)
            if vt_blk.shape[1] > p.shape[0]:
                p = jnp.concatenate([p, jnp.zeros((vt_blk.shape[1] - p.shape[0], n), BF16)], axis=0)
            probs.append(p)
            m_scr[h:h + 1, lanes] = m_new
            l_scr[h:h + 1, lanes] = p_sum if first else alphas[h] * l_scr[h:h + 1, lanes] + p_sum
        for h in range(n_heads):
            head_rows = slice(h * HEAD_DIM, (h + 1) * HEAD_DIM)
            pv = _mxu_dot(vt_blk[head_rows, :], probs[h])
            acc_scr[head_rows, lanes] = pv if first else alphas[h] * acc_scr[head_rows, lanes] + pv

    meta_rows = kmeta_ref.at[0:N_META, :]
    fold(lambda h: qk(meta_rows, h, all_q), vtmeta_ref, None, True, all_q)

    stage_scores(0, 0)

    def two_chunks(jj, carry):
        c = 2 * jj
        stage_scores(c + 1, 1)
        fold(lambda h: s_scr[0, h], vt_ref.at[c], None, False, all_q)
        stage_scores(c + 2, 0)
        fold(lambda h: s_scr[1, h], vt_ref.at[c + 1], None, False, all_q)
        return carry

    lax.fori_loop(0, i, two_chunks, 0)

    c = 2 * i
    last_rows = key_rows(c + 1)
    late_scores = [qk(last_rows, h, late_q) for h in range(n_heads)]
    causal_all = (lax.broadcasted_iota(jnp.int32, (tk, tq), 0)
                  <= lax.broadcasted_iota(jnp.int32, (tk, tq), 1))
    fold(lambda h: s_scr[0, h], vt_ref.at[c], causal_all, False, all_q)
    fold(lambda h: late_scores[h], vt_ref.at[c + 1], causal_all[:, 0:tk], False, late_q)

    for h in range(n_heads):
        head_rows = slice(h * HEAD_DIM, (h + 1) * HEAD_DIM)
        o = acc_scr[head_rows, :] / l_scr[h:h + 1, :]
        acc_scr[head_rows, :] = o * lax.rsqrt(jnp.mean(o * o, axis=0, keepdims=True) + EPS)
    y_ref[...] = (acc_scr[...].T * gattn_ref[...]).astype(BF16)


def _attention_call(q, kaug, vt, kaug_meta, vt_meta, gattn, *, batch, seq, tq):
    cw = q.shape[1]
    tk = vt.shape[2]
    assert tq == 2 * tk, "the diagonal handling assumes two key chunks per query tile"
    chunks = seq // tk
    n_heads = cw // HEAD_DIM
    vt4 = vt.reshape(batch, chunks, cw, tk)
    q_tiles = seq // tq

    def const(shape):
        return pl.BlockSpec(shape, lambda b, i: (0,) * len(shape))

    return pl.pallas_call(
        _attention_body,
        grid=(batch, q_tiles),
        in_specs=[pl.BlockSpec((tq, cw), lambda b, i: (b * q_tiles + i, 0)),
                  pl.BlockSpec((seq, 2 * cw), lambda b, i: (b, 0)),
                  pl.BlockSpec((None, chunks, cw, tk), lambda b, i: (b, 0, 0, 0)),
                  const(kaug_meta.shape), const(vt_meta.shape), const(gattn.shape)],
        out_specs=pl.BlockSpec((tq, cw), lambda b, i: (b * q_tiles + i, 0)),
        out_shape=jax.ShapeDtypeStruct(q.shape, BF16),
        scratch_shapes=[pltpu.VMEM((n_heads, 2 * HEAD_PAIR, tq), BF16),
                        pltpu.VMEM((2, n_heads, tk, tq), F32),
                        pltpu.VMEM((n_heads, tq), F32),
                        pltpu.VMEM((n_heads, tq), F32),
                        pltpu.VMEM((cw, tq), F32)],
        compiler_params=pltpu.CompilerParams(dimension_semantics=("parallel", "parallel"),
                                             vmem_limit_bytes=_vmem_limit()),
        name="fox_attention",
    )(q, kaug, vt4, kaug_meta, vt_meta, gattn)


def _post_mixer_body(h1_ref, yc_ref, ya_ref, wout_ref, g2_ref, wgu_ref, wdown_ref, gf_ref, out_ref, act_scr):
    cw = yc_ref.shape[1]
    parts = _row_parts(h1_ref.shape[0])
    h2 = [h1_ref[r, :]
          + _mxu_dot(yc_ref[r, :], wout_ref[0:cw, :])
          + _mxu_dot(ya_ref[r, :], wout_ref[cw:2 * cw, :]) for r in parts]
    n = [_rms_norm(h, g2_ref[...]).astype(BF16) for h in h2]
    ffn = _swiglu_ffn(n, parts, wgu_ref, wdown_ref, act_scr)
    for h, f, r in zip(h2, ffn, parts):
        out_ref[r, :] = _rms_norm(h + 0.5 * f, gf_ref[...])


def _post_mixer_call(h1, yconv, yattn, wout, g2, wgu, wdown, gf, *, tm):
    n_rows, d = h1.shape
    cw = yconv.shape[1]
    d_ff = wdown.shape[0]

    def resident(shape):
        return pl.BlockSpec(shape, lambda t: (0,) * len(shape), pipeline_mode=pl.Buffered(1))

    def rows(width):
        return pl.BlockSpec((tm, width), lambda t: (t, 0))

    return pl.pallas_call(
        _post_mixer_body,
        grid=(n_rows // tm,),
        in_specs=[rows(d), rows(cw), rows(cw), resident(wout.shape), resident(g2.shape),
                  resident(wgu.shape), resident(wdown.shape), resident(gf.shape)],
        out_specs=rows(d),
        out_shape=jax.ShapeDtypeStruct((n_rows, d), F32),
        scratch_shapes=[pltpu.VMEM((tm, d_ff), BF16)],
        compiler_params=pltpu.CompilerParams(dimension_semantics=("parallel",),
                                             vmem_limit_bytes=_vmem_limit()),
        name="post_mixer",
    )(h1, yconv, yattn, wout, g2, wgu, wdown, gf)


def _vmem_limit():
    return V7X_VMEM_BYTES * 7 // 8


def _layer(x, meta_tokens, ffn1_norm, ffn1_w_gu, ffn1_w_down, mix_norm, w_in, conv_w, b_f, out_norm_conv,
           out_norm_attn, w_out, ffn2_norm, ffn2_w_gu, ffn2_w_down, final_norm, *, tm, tq):
    batch, seq, d = x.shape
    assert ffn1_norm.shape[0] == 1, "one layer deep"
    assert seq % tm == 0 and seq % tq == 0 and tm % ATTN_KEY_CHUNK == 0
    cw = conv_w.shape[2]
    n_heads = b_f.shape[1]
    row = lambda a: a.reshape(1, -1).astype(F32)

    wgu1, wdown1 = ffn1_w_gu[0].astype(BF16), ffn1_w_down[0].astype(BF16)
    wgu2, wdown2 = ffn2_w_gu[0].astype(BF16), ffn2_w_down[0].astype(BF16)
    win = w_in[0, :, :6 * cw].astype(BF16)
    wfg = jnp.pad(w_in[0, :, 6 * cw:], ((0, 0), (0, V7X_LANES - n_heads))).astype(BF16)
    bfp = jnp.pad(row(b_f[0]), ((0, 0), (0, V7X_LANES - n_heads)))
    wout = w_out[0].astype(BF16)
    pre_args = (row(ffn1_norm[0]), wgu1, wdown1, row(mix_norm[0]), win, wfg, bfp, conv_w[0].astype(F32),
                row(out_norm_conv[0]))

    meta_rows = V7X_LANES
    x_meta = jnp.pad(meta_tokens.astype(F32), ((0, meta_rows - N_META), (0, 0)))
    _, _, kaug_meta, vt_meta, _, u_meta, g_meta = _pre_mixer_call(
        x_meta, *pre_args, jnp.zeros((V7X_SUBLANES, cw), F32), jnp.zeros((n_heads, V7X_LANES), F32),
        tm=meta_rows, tiles_per_seq=1, tail_start=N_META - V7X_SUBLANES, tk=meta_rows, n_heads=n_heads)

    h1, q, kaug, vt, yconv, _, _ = _pre_mixer_call(
        x.reshape(batch * seq, d), *pre_args, u_meta, g_meta,
        tm=tm, tiles_per_seq=seq // tm, tail_start=tm - V7X_SUBLANES, tk=ATTN_KEY_CHUNK, n_heads=n_heads)

    yattn = _attention_call(q, kaug, vt, kaug_meta, vt_meta.reshape(cw, meta_rows),
                            row(out_norm_attn[0]), batch=batch, seq=seq, tq=tq)

    out = _post_mixer_call(h1, yconv, yattn, wout, row(ffn2_norm[0]), wgu2, wdown2, row(final_norm), tm=tm)
    return out.reshape(batch, seq, d)


def kernel(x, meta_tokens, ffn1_norm, ffn1_w_gu, ffn1_w_down, mix_norm, w_in, conv_w, b_f, out_norm_conv,
           out_norm_attn, w_out, ffn2_norm, ffn2_w_gu, ffn2_w_down, final_norm):
    return _layer(x, meta_tokens, ffn1_norm, ffn1_w_gu, ffn1_w_down, mix_norm, w_in, conv_w, b_f,
                  out_norm_conv, out_norm_attn, w_out, ffn2_norm, ffn2_w_gu, ffn2_w_down, final_norm,
                  tm=TOKEN_TILE, tq=ATTN_TILE)
```

```python
import functools
import math

import jax
import jax.numpy as jnp
from jax import lax
from jax.experimental import pallas as pl
from jax.experimental.pallas import tpu as pltpu

EPS = 1e-6
LOG2_E = math.log2(math.e)
N_META = 16
HEAD_DIM = 64
CONV_K = 3

V7X_LANES = 128
V7X_SUBLANES = 8
V7X_MXU_DIM = 256
V7X_VMEM_BYTES = 64 * 1024 * 1024

TOKEN_TILE = 512
ROW_SUBTILE = 256
ATTN_KEY_CHUNK = V7X_MXU_DIM
ATTN_TILE = 2 * ATTN_KEY_CHUNK
FF_CHUNK = V7X_MXU_DIM
HEAD_PAIR = 2 * HEAD_DIM
V7X_BF16_ROW_PACK = 2 * V7X_SUBLANES
HEAD_SLOT = HEAD_DIM + V7X_BF16_ROW_PACK

F32 = jnp.float32
BF16 = jnp.bfloat16


def _mxu_dot(a, b):
    return jnp.dot(a, b, preferred_element_type=F32)


def _rms_norm(x, g):
    return x * lax.rsqrt(jnp.mean(x * x, axis=-1, keepdims=True) + EPS) * g


def _row_parts(tm):
    sub = min(tm, ROW_SUBTILE)
    return [slice(r, r + sub) for r in range(0, tm, sub)]


def _swiglu_ffn(ns, parts, wgu_ref, wdown_ref, act_scr):
    d_ff = wdown_ref.shape[0]
    for n, r in zip(ns, parts):
        for lo in range(0, d_ff, FF_CHUNK):
            g = _mxu_dot(n, wgu_ref[:, lo:lo + FF_CHUNK])
            u = _mxu_dot(n, wgu_ref[:, d_ff + lo:d_ff + lo + FF_CHUNK])
            act_scr[r, lo:lo + FF_CHUNK] = (g * jax.nn.sigmoid(g) * u).astype(BF16)
    return [_mxu_dot(act_scr[r, :], wdown_ref[...]) for r in parts]


def _group_norm_lanes(y, g):
    lane = lax.broadcasted_iota(jnp.int32, (y.shape[0], V7X_LANES), 1)
    low = lane < HEAD_DIM
    outs = []
    for c in range(0, y.shape[1], V7X_LANES):
        blk = y[:, c:c + V7X_LANES]
        sq = blk * blk
        ms_lo = jnp.sum(jnp.where(low, sq, 0.0), axis=-1, keepdims=True) * (1.0 / HEAD_DIM)
        ms_hi = jnp.sum(jnp.where(low, 0.0, sq), axis=-1, keepdims=True) * (1.0 / HEAD_DIM)
        inv = jnp.where(low, lax.rsqrt(ms_lo + EPS), lax.rsqrt(ms_hi + EPS))
        outs.append(blk * inv * g[:, c:c + V7X_LANES])
    return jnp.concatenate(outs, axis=-1)


def _log_sigmoid(x):
    z = -x
    return -(jnp.maximum(z, 0.0) + jnp.log1p(jnp.exp(-jnp.abs(z))))


def _cumsum_lanes(a):
    r, n = a.shape
    lane = lax.broadcasted_iota(jnp.int32, (r, V7X_LANES), 1)
    blocks = []
    offset = jnp.zeros((r, 1), F32)
    for c in range(0, n, V7X_LANES):
        blk = a[:, c:c + V7X_LANES]
        shift = 1
        while shift < V7X_LANES:
            blk = blk + jnp.where(lane >= shift, pltpu.roll(blk, shift, axis=1), 0.0)
            shift *= 2
        blk = blk + offset
        offset = blk[:, V7X_LANES - 1:V7X_LANES]
        blocks.append(blk)
    return jnp.concatenate(blocks, axis=-1)


def _gate_pieces(g, n_heads):
    lane = lax.broadcasted_iota(jnp.int32, g.shape, 1)
    hi = g.astype(BF16).astype(F32)
    rest = g - hi
    mid = rest.astype(BF16).astype(F32)
    low = rest - mid
    pieces = jnp.where(lane < n_heads, hi,
                       jnp.where(lane < 2 * n_heads, pltpu.roll(mid, n_heads, axis=1),
                                 jnp.where(lane < 3 * n_heads, pltpu.roll(low, 2 * n_heads, axis=1), 0.0)))
    return pieces.astype(BF16)


def _pre_mixer_body(x_ref, g1_ref, wgu_ref, wdown_ref, gm_ref, win_ref, wfg_ref, bf_ref, convw_ref,
                    gconv_ref, uinit_ref, ginit_ref,
                    h1_ref, q_ref, kaug_ref, vt_ref, yconv_ref, utail_ref, gtail_ref,
                    act_scr, u_scr, gcarry_scr, *, tiles_per_seq, tail_start, n_heads):
    tm = x_ref.shape[0]
    cw = q_ref.shape[1]
    halo = V7X_SUBLANES

    @pl.when(pl.program_id(0) % tiles_per_seq == 0)
    def _():
        u_scr[0:halo, :] = uinit_ref[...]
        gcarry_scr[...] = ginit_ref[...]

    parts = _row_parts(tm)
    sub = parts[0].stop
    xs = [x_ref[r, :] for r in parts]
    n1 = [_rms_norm(x, g1_ref[...]).astype(BF16) for x in xs]
    ffn = _swiglu_ffn(n1, parts, wgu_ref, wdown_ref, act_scr)
    n2 = []
    for x, f, r in zip(xs, ffn, parts):
        h1 = x + 0.5 * f
        h1_ref[r, :] = h1
        n2.append(_rms_norm(h1, gm_ref[...]).astype(BF16))

    def proj(n, j):
        return _mxu_dot(n, win_ref[:, j * cw:(j + 1) * cw])

    tail_col = tail_start + halo - 1
    g_carry = gcarry_scr[0:n_heads, 0:1]
    gates = []
    for n, r in zip(n2, parts):
        fg = _mxu_dot(n, wfg_ref[...]) + bf_ref[...]
        g_run = _cumsum_lanes(_log_sigmoid(fg).T[0:n_heads, :]) + g_carry
        g_carry = g_run[:, sub - 1:sub]
        if r.start <= tail_col < r.stop:
            gtail_ref[...] = jnp.broadcast_to(g_run[:, tail_col - r.start:tail_col - r.start + 1],
                                              gtail_ref.shape)
        g_rows = jnp.concatenate([g_run * LOG2_E, jnp.zeros((V7X_LANES - n_heads, sub), F32)], axis=0)
        gates.append(_gate_pieces(g_rows.T, n_heads))
    gcarry_scr[...] = jnp.broadcast_to(g_carry, gcarry_scr.shape)

    us = [proj(n, 1) * proj(n, 2) for n in n2]
    for u, r in zip(us, parts):
        u_scr[halo + r.start:halo + r.stop, :] = u
    for n, u, r in zip(n2, us, parts):
        conv = (convw_ref[0:1, :] * u_scr[pl.ds(halo - 2 + r.start, sub), :]
                + convw_ref[1:2, :] * u_scr[pl.ds(halo - 1 + r.start, sub), :]
                + convw_ref[2:3, :] * u)
        yconv_ref[r, :] = _group_norm_lanes(proj(n, 0) * conv, gconv_ref[...]).astype(BF16)
    utail_ref[...] = u_scr[halo + tail_start:halo + tail_start + halo, :]
    u_scr[0:halo, :] = u_scr[tm:tm + halo, :]

    tk = vt_ref.shape[2]
    pad_rows = HEAD_SLOT - HEAD_DIM
    ones_row = jnp.where(lax.broadcasted_iota(jnp.int32, (pad_rows, tk), 0) == 0, 1.0, 0.0).astype(BF16)
    for n, r in zip(n2, parts):
        q_ref[r, :] = (proj(n, 3) * (HEAD_DIM ** -0.5 * LOG2_E)).astype(BF16)
        vt = proj(n, 5).T
        for c in range(sub // tk):
            chunk = r.start // tk + c
            for h in range(n_heads):
                vt_ref[chunk, h * HEAD_SLOT:h * HEAD_SLOT + HEAD_DIM, :] = (
                    vt[h * HEAD_DIM:(h + 1) * HEAD_DIM, c * tk:(c + 1) * tk].astype(BF16))
                vt_ref[chunk, h * HEAD_SLOT + HEAD_DIM:(h + 1) * HEAD_SLOT, :] = ones_row

    for n, gate, r in zip(n2, gates, parts):
        k = proj(n, 4).astype(BF16)
        for pair in range(cw // HEAD_PAIR):
            base = pair * 2 * HEAD_PAIR
            kaug_ref[r, base:base + HEAD_PAIR] = k[:, pair * HEAD_PAIR:(pair + 1) * HEAD_PAIR]
            kaug_ref[r, base + HEAD_PAIR:base + 2 * HEAD_PAIR] = gate


def _pre_mixer_call(x2, g1, wgu, wdown, gm, win, wfg, bfp, convw, gconv, uinit, ginit, *, tm, tiles_per_seq,
                    tail_start, tk, n_heads):
    n_rows, d = x2.shape
    cw = gconv.shape[1]
    d_ff = wdown.shape[0]
    n_tiles = n_rows // tm
    halo = V7X_SUBLANES

    def resident(shape):
        return pl.BlockSpec(shape, lambda t: (0,) * len(shape), pipeline_mode=pl.Buffered(1))

    def rows(width):
        return pl.BlockSpec((tm, width), lambda t: (t, 0))

    body = functools.partial(_pre_mixer_body, tiles_per_seq=tiles_per_seq, tail_start=tail_start,
                             n_heads=n_heads)
    return pl.pallas_call(
        body,
        grid=(n_tiles,),
        in_specs=[rows(d), resident(g1.shape), resident(wgu.shape), resident(wdown.shape),
                  resident(gm.shape), resident(win.shape), resident(wfg.shape), resident(bfp.shape),
                  resident(convw.shape), resident(gconv.shape), resident(uinit.shape),
                  resident(ginit.shape)],
        out_specs=[rows(d), rows(cw), rows(2 * cw),
                   pl.BlockSpec((tm // tk, n_heads * HEAD_SLOT, tk), lambda t: (t, 0, 0)),
                   rows(cw),
                   pl.BlockSpec((halo, cw), lambda t: (t, 0)),
                   pl.BlockSpec((n_heads, V7X_LANES), lambda t: (t, 0))],
        out_shape=[jax.ShapeDtypeStruct((n_rows, d), F32),
                   jax.ShapeDtypeStruct((n_rows, cw), BF16),
                   jax.ShapeDtypeStruct((n_rows, 2 * cw), BF16),
                   jax.ShapeDtypeStruct((n_rows // tk, n_heads * HEAD_SLOT, tk), BF16),
                   jax.ShapeDtypeStruct((n_rows, cw), BF16),
                   jax.ShapeDtypeStruct((n_tiles * halo, cw), F32),
                   jax.ShapeDtypeStruct((n_tiles * n_heads, V7X_LANES), F32)],
        scratch_shapes=[pltpu.VMEM((tm, d_ff), BF16),
                        pltpu.VMEM((tm + halo, cw), F32),
                        pltpu.VMEM((n_heads, V7X_LANES), F32)],
        compiler_params=pltpu.CompilerParams(dimension_semantics=("arbitrary",),
                                             vmem_limit_bytes=_vmem_limit()),
        name="pre_mixer",
    )(x2, g1, wgu, wdown, gm, win, wfg, bfp, convw, gconv, uinit, ginit)


def _attention_body(q_ref, kaug_ref, vt_ref, kmeta_ref, vtmeta_ref, gattn_ref, y_ref,
                    qm_scr, s_scr, m_scr, acc_scr, out_scr):
    tq = q_ref.shape[0]
    tk = vt_ref.shape[2]
    n_heads = q_ref.shape[1] // HEAD_DIM
    i = pl.program_id(1)
    neg_inf = float("-inf")

    qt = q_ref[...].astype(F32).T.astype(BF16)
    pair_row = lax.broadcasted_iota(jnp.int32, (HEAD_PAIR, tq), 0)
    for h in range(n_heads):
        pair, half = divmod(h, 2)
        mine = (pair_row >= half * HEAD_DIM) & (pair_row < (half + 1) * HEAD_DIM)
        picks = (pair_row == h) | (pair_row == n_heads + h) | (pair_row == 2 * n_heads + h)
        qm_scr[h, 0:HEAD_PAIR, :] = jnp.where(mine, qt[pair * HEAD_PAIR:(pair + 1) * HEAD_PAIR, :],
                                               jnp.zeros((), BF16))
        qm_scr[h, HEAD_PAIR:2 * HEAD_PAIR, :] = jnp.where(picks, -1.0, 0.0).astype(BF16)

    all_q = slice(0, tq)
    late_q = slice(tq - tk, tq)

    def qk(kaug_rows, h, lanes):
        width = 2 * HEAD_PAIR
        return _mxu_dot(kaug_rows[:, (h // 2) * width:(h // 2 + 1) * width], qm_scr[h, :, lanes])

    def key_rows(c):
        return kaug_ref.at[pl.ds(pl.multiple_of(c * tk, tk), tk), :]

    def fold_head(h, s, vt_blk, mask, first, lanes):
        n = lanes.stop - lanes.start
        slot_rows = slice(h * HEAD_SLOT, (h + 1) * HEAD_SLOT)
        if mask is not None:
            masked = jnp.where(mask, s[:, 0:mask.shape[1]], neg_inf)
            s = masked if mask.shape[1] == n else jnp.concatenate([masked, s[:, mask.shape[1]:]], axis=1)
        c_max = jnp.max(s, axis=0, keepdims=True)
        if first:
            m_new = c_max
        else:
            m_old = m_scr[h:h + 1, lanes]
            m_new = jnp.maximum(m_old, c_max)
            alpha = jnp.exp2(m_old - m_new)
        p = jnp.exp2(s - m_new).astype(BF16)
        if vt_blk.shape[1] > p.shape[0]:
            p = jnp.concatenate([p, jnp.zeros((vt_blk.shape[1] - p.shape[0], n), BF16)], axis=0)
        m_scr[h:h + 1, lanes] = m_new
        pv = _mxu_dot(vt_blk[slot_rows, :], p)
        acc_scr[slot_rows, lanes] = pv if first else alpha * acc_scr[slot_rows, lanes] + pv

    def stage_and_fold(c_next, slot_next, fold_one):
        rows = key_rows(c_next)
        for h in range(n_heads):
            s_scr[slot_next, h] = qk(rows, h, all_q)
            fold_one(h)

    meta_rows = kmeta_ref.at[0:N_META, :]
    meta_scores = [qk(meta_rows, h, all_q) for h in range(n_heads)]
    stage_and_fold(0, 0, lambda h: None)
    for h in range(n_heads):
        fold_head(h, meta_scores[h], vtmeta_ref, None, True, all_q)

    def two_chunks(jj, carry):
        c = 2 * jj
        stage_and_fold(c + 1, 1, lambda h: fold_head(h, s_scr[0, h], vt_ref.at[c], None, False, all_q))
        stage_and_fold(c + 2, 0, lambda h: fold_head(h, s_scr[1, h], vt_ref.at[c + 1], None, False, all_q))
        return carry

    lax.fori_loop(0, i, two_chunks, 0)

    c = 2 * i
    last_rows = key_rows(c + 1)
    causal = (lax.broadcasted_iota(jnp.int32, (tk, tk), 0)
              <= lax.broadcasted_iota(jnp.int32, (tk, tk), 1))
    late_scores = []
    for h in range(n_heads):
        late_scores.append(qk(last_rows, h, late_q))
        fold_head(h, s_scr[0, h], vt_ref.at[c], causal, False, all_q)
    for h in range(n_heads):
        fold_head(h, late_scores[h], vt_ref.at[c + 1], causal, False, late_q)

    for h in range(n_heads):
        base = h * HEAD_SLOT
        o = acc_scr[base:base + HEAD_DIM, :] / acc_scr[base + HEAD_DIM:base + HEAD_DIM + 1, :]
        out_scr[h * HEAD_DIM:(h + 1) * HEAD_DIM, :] = (
            o * lax.rsqrt(jnp.mean(o * o, axis=0, keepdims=True) + EPS))
    y_ref[...] = (out_scr[...].T * gattn_ref[...]).astype(BF16)


def _attention_call(q, kaug, vt, kaug_meta, vt_meta, gattn, *, batch, seq, tq):
    cw = q.shape[1]
    tk = vt.shape[2]
    assert tq == 2 * tk, "the diagonal handling assumes two key chunks per query tile"
    chunks = seq // tk
    n_heads = cw // HEAD_DIM
    slots = vt.shape[1]
    vt4 = vt.reshape(batch, chunks, slots, tk)
    q_tiles = seq // tq

    def const(shape):
        return pl.BlockSpec(shape, lambda b, i: (0,) * len(shape))

    return pl.pallas_call(
        _attention_body,
        grid=(batch, q_tiles),
        in_specs=[pl.BlockSpec((tq, cw), lambda b, i: (b * q_tiles + i, 0)),
                  pl.BlockSpec((seq, 2 * cw), lambda b, i: (b, 0)),
                  pl.BlockSpec((None, chunks, slots, tk), lambda b, i: (b, 0, 0, 0)),
                  const(kaug_meta.shape), const(vt_meta.shape), const(gattn.shape)],
        out_specs=pl.BlockSpec((tq, cw), lambda b, i: (b * q_tiles + i, 0)),
        out_shape=jax.ShapeDtypeStruct(q.shape, BF16),
        scratch_shapes=[pltpu.VMEM((n_heads, 2 * HEAD_PAIR, tq), BF16),
                        pltpu.VMEM((2, n_heads, tk, tq), F32),
                        pltpu.VMEM((n_heads, tq), F32),
                        pltpu.VMEM((slots, tq), F32),
                        pltpu.VMEM((cw, tq), F32)],
        compiler_params=pltpu.CompilerParams(dimension_semantics=("parallel", "parallel"),
                                             vmem_limit_bytes=_vmem_limit()),
        name="fox_attention",
    )(q, kaug, vt4, kaug_meta, vt_meta, gattn)


def _post_mixer_body(h1_ref, yc_ref, ya_ref, wout_ref, g2_ref, wgu_ref, wdown_ref, gf_ref, out_ref, act_scr):
    cw = yc_ref.shape[1]
    parts = _row_parts(h1_ref.shape[0])
    h2 = [h1_ref[r, :]
          + _mxu_dot(yc_ref[r, :], wout_ref[0:cw, :])
          + _mxu_dot(ya_ref[r, :], wout_ref[cw:2 * cw, :]) for r in parts]
    n = [_rms_norm(h, g2_ref[...]).astype(BF16) for h in h2]
    ffn = _swiglu_ffn(n, parts, wgu_ref, wdown_ref, act_scr)
    for h, f, r in zip(h2, ffn, parts):
        out_ref[r, :] = _rms_norm(h + 0.5 * f, gf_ref[...])


def _post_mixer_call(h1, yconv, yattn, wout, g2, wgu, wdown, gf, *, tm):
    n_rows, d = h1.shape
    cw = yconv.shape[1]
    d_ff = wdown.shape[0]

    def resident(shape):
        return pl.BlockSpec(shape, lambda t: (0,) * len(shape), pipeline_mode=pl.Buffered(1))

    def rows(width):
        return pl.BlockSpec((tm, width), lambda t: (t, 0))

    return pl.pallas_call(
        _post_mixer_body,
        grid=(n_rows // tm,),
        in_specs=[rows(d), rows(cw), rows(cw), resident(wout.shape), resident(g2.shape),
                  resident(wgu.shape), resident(wdown.shape), resident(gf.shape)],
        out_specs=rows(d),
        out_shape=jax.ShapeDtypeStruct((n_rows, d), F32),
        scratch_shapes=[pltpu.VMEM((tm, d_ff), BF16)],
        compiler_params=pltpu.CompilerParams(dimension_semantics=("parallel",),
                                             vmem_limit_bytes=_vmem_limit()),
        name="post_mixer",
    )(h1, yconv, yattn, wout, g2, wgu, wdown, gf)


def _vmem_limit():
    return V7X_VMEM_BYTES * 7 // 8


def _layer(x, meta_tokens, ffn1_norm, ffn1_w_gu, ffn1_w_down, mix_norm, w_in, conv_w, b_f, out_norm_conv,
           out_norm_attn, w_out, ffn2_norm, ffn2_w_gu, ffn2_w_down, final_norm, *, tm, tq):
    batch, seq, d = x.shape
    assert ffn1_norm.shape[0] == 1, "one layer deep"
    assert seq % tm == 0 and seq % tq == 0 and tm % ATTN_KEY_CHUNK == 0
    cw = conv_w.shape[2]
    n_heads = b_f.shape[1]
    row = lambda a: a.reshape(1, -1).astype(F32)

    wgu1, wdown1 = ffn1_w_gu[0].astype(BF16), ffn1_w_down[0].astype(BF16)
    wgu2, wdown2 = ffn2_w_gu[0].astype(BF16), ffn2_w_down[0].astype(BF16)
    win = w_in[0, :, :6 * cw].astype(BF16)
    wfg = jnp.pad(w_in[0, :, 6 * cw:], ((0, 0), (0, V7X_LANES - n_heads))).astype(BF16)
    bfp = jnp.pad(row(b_f[0]), ((0, 0), (0, V7X_LANES - n_heads)))
    wout = w_out[0].astype(BF16)
    pre_args = (row(ffn1_norm[0]), wgu1, wdown1, row(mix_norm[0]), win, wfg, bfp, conv_w[0].astype(F32),
                row(out_norm_conv[0]))

    meta_rows = V7X_LANES
    x_meta = jnp.pad(meta_tokens.astype(F32), ((0, meta_rows - N_META), (0, 0)))
    _, _, kaug_meta, vt_meta, _, u_meta, g_meta = _pre_mixer_call(
        x_meta, *pre_args, jnp.zeros((V7X_SUBLANES, cw), F32), jnp.zeros((n_heads, V7X_LANES), F32),
        tm=meta_rows, tiles_per_seq=1, tail_start=N_META - V7X_SUBLANES, tk=meta_rows, n_heads=n_heads)

    h1, q, kaug, vt, yconv, _, _ = _pre_mixer_call(
        x.reshape(batch * seq, d), *pre_args, u_meta, g_meta,
        tm=tm, tiles_per_seq=seq // tm, tail_start=tm - V7X_SUBLANES, tk=ATTN_KEY_CHUNK, n_heads=n_heads)

    yattn = _attention_call(q, kaug, vt, kaug_meta, vt_meta.reshape(n_heads * HEAD_SLOT, meta_rows),
                            row(out_norm_attn[0]), batch=batch, seq=seq, tq=tq)

    out = _post_mixer_call(h1, yconv, yattn, wout, row(ffn2_norm[0]), wgu2, wdown2, row(final_norm), tm=tm)
    return out.reshape(batch, seq, d)


def kernel(x, meta_tokens, ffn1_norm, ffn1_w_gu, ffn1_w_down, mix_norm, w_in, conv_w, b_f, out_norm_conv,
           out_norm_attn, w_out, ffn2_norm, ffn2_w_gu, ffn2_w_down, final_norm):
    return _layer(x, meta_tokens, ffn1_norm, ffn1_w_gu, ffn1_w_down, mix_norm, w_in, conv_w, b_f,
                  out_norm_conv, out_norm_attn, w_out, ffn2_norm, ffn2_w_gu, ffn2_w_down, final_norm,
                  tm=TOKEN_TILE, tq=ATTN_TILE)
```

```python
import functools
import math

import jax
import jax.numpy as jnp
from jax import lax
from jax.experimental import pallas as pl
from jax.experimental.pallas import tpu as pltpu

EPS = 1e-6
LOG2_E = math.log2(math.e)
N_META = 16
HEAD_DIM = 64
CONV_K = 3

V7X_LANES = 128
V7X_SUBLANES = 8
V7X_MXU_DIM = 256
V7X_VMEM_BYTES = 64 * 1024 * 1024

TOKEN_TILE = 512
ROW_SUBTILE = 256
ATTN_KEY_CHUNK = V7X_MXU_DIM
ATTN_TILE = 2 * ATTN_KEY_CHUNK
FF_CHUNK = V7X_MXU_DIM
HEAD_PAIR = 2 * HEAD_DIM
V7X_BF16_ROW_PACK = 2 * V7X_SUBLANES
HEAD_SLOT = HEAD_DIM + V7X_BF16_ROW_PACK

F32 = jnp.float32
BF16 = jnp.bfloat16


def _mxu_dot(a, b):
    return jnp.dot(a, b, preferred_element_type=F32)


def _rms_norm(x, g):
    return x * lax.rsqrt(jnp.mean(x * x, axis=-1, keepdims=True) + EPS) * g


def _row_parts(tm):
    sub = min(tm, ROW_SUBTILE)
    return [slice(r, r + sub) for r in range(0, tm, sub)]


def _swiglu_ffn(ns, parts, wgu_ref, wdown_ref, act_scr):
    d_ff = wdown_ref.shape[0]
    for n, r in zip(ns, parts):
        for lo in range(0, d_ff, FF_CHUNK):
            g = _mxu_dot(n, wgu_ref[:, lo:lo + FF_CHUNK])
            u = _mxu_dot(n, wgu_ref[:, d_ff + lo:d_ff + lo + FF_CHUNK])
            act_scr[r, lo:lo + FF_CHUNK] = (g * jax.nn.sigmoid(g) * u).astype(BF16)
    return [_mxu_dot(act_scr[r, :], wdown_ref[...]) for r in parts]


def _group_norm_lanes(y, g):
    lane = lax.broadcasted_iota(jnp.int32, (y.shape[0], V7X_LANES), 1)
    low = lane < HEAD_DIM
    outs = []
    for c in range(0, y.shape[1], V7X_LANES):
        blk = y[:, c:c + V7X_LANES]
        sq = blk * blk
        ms_lo = jnp.sum(jnp.where(low, sq, 0.0), axis=-1, keepdims=True) * (1.0 / HEAD_DIM)
        ms_hi = jnp.sum(jnp.where(low, 0.0, sq), axis=-1, keepdims=True) * (1.0 / HEAD_DIM)
        inv = jnp.where(low, lax.rsqrt(ms_lo + EPS), lax.rsqrt(ms_hi + EPS))
        outs.append(blk * inv * g[:, c:c + V7X_LANES])
    return jnp.concatenate(outs, axis=-1)


def _log_sigmoid(x):
    z = -x
    return -(jnp.maximum(z, 0.0) + jnp.log1p(jnp.exp(-jnp.abs(z))))


def _cumsum_lanes(a):
    r, n = a.shape
    lane = lax.broadcasted_iota(jnp.int32, (r, V7X_LANES), 1)
    blocks = []
    offset = jnp.zeros((r, 1), F32)
    for c in range(0, n, V7X_LANES):
        blk = a[:, c:c + V7X_LANES]
        shift = 1
        while shift < V7X_LANES:
            blk = blk + jnp.where(lane >= shift, pltpu.roll(blk, shift, axis=1), 0.0)
            shift *= 2
        blk = blk + offset
        offset = blk[:, V7X_LANES - 1:V7X_LANES]
        blocks.append(blk)
    return jnp.concatenate(blocks, axis=-1)


def _gate_pieces(g, n_heads):
    lane = lax.broadcasted_iota(jnp.int32, g.shape, 1)
    hi = g.astype(BF16).astype(F32)
    rest = g - hi
    mid = rest.astype(BF16).astype(F32)
    low = rest - mid
    pieces = jnp.where(lane < n_heads, hi,
                       jnp.where(lane < 2 * n_heads, pltpu.roll(mid, n_heads, axis=1),
                                 jnp.where(lane < 3 * n_heads, pltpu.roll(low, 2 * n_heads, axis=1), 0.0)))
    return pieces.astype(BF16)


def _pre_mixer_body(x_ref, g1_ref, wgu_ref, wdown_ref, gm_ref, win_ref, wfg_ref, bf_ref, convw_ref,
                    gconv_ref, uinit_ref, ginit_ref,
                    h1_ref, q_ref, kaug_ref, vt_ref, yconv_ref, utail_ref, gtail_ref,
                    act_scr, u_scr, gcarry_scr, *, tiles_per_seq, tail_start, n_heads):
    tm = x_ref.shape[0]
    cw = q_ref.shape[1]
    halo = V7X_SUBLANES

    @pl.when(pl.program_id(0) % tiles_per_seq == 0)
    def _():
        u_scr[0:halo, :] = uinit_ref[...]
        gcarry_scr[...] = ginit_ref[...]

    parts = _row_parts(tm)
    sub = parts[0].stop
    xs = [x_ref[r, :] for r in parts]
    n1 = [_rms_norm(x, g1_ref[...]).astype(BF16) for x in xs]
    ffn = _swiglu_ffn(n1, parts, wgu_ref, wdown_ref, act_scr)
    n2 = []
    for x, f, r in zip(xs, ffn, parts):
        h1 = x + 0.5 * f
        h1_ref[r, :] = h1
        n2.append(_rms_norm(h1, gm_ref[...]).astype(BF16))

    def proj(n, j):
        return _mxu_dot(n, win_ref[:, j * cw:(j + 1) * cw])

    tail_col = tail_start + halo - 1
    g_carry = gcarry_scr[0:n_heads, 0:1]
    gates = []
    for n, r in zip(n2, parts):
        fg = _mxu_dot(n, wfg_ref[...]) + bf_ref[...]
        g_run = _cumsum_lanes(_log_sigmoid(fg).T[0:n_heads, :]) + g_carry
        g_carry = g_run[:, sub - 1:sub]
        if r.start <= tail_col < r.stop:
            gtail_ref[...] = jnp.broadcast_to(g_run[:, tail_col - r.start:tail_col - r.start + 1],
                                              gtail_ref.shape)
        g_rows = jnp.concatenate([g_run * LOG2_E, jnp.zeros((V7X_LANES - n_heads, sub), F32)], axis=0)
        gates.append(_gate_pieces(g_rows.T, n_heads))
    gcarry_scr[...] = jnp.broadcast_to(g_carry, gcarry_scr.shape)

    us = [proj(n, 1) * proj(n, 2) for n in n2]
    for u, r in zip(us, parts):
        u_scr[halo + r.start:halo + r.stop, :] = u
    for n, u, r in zip(n2, us, parts):
        conv = (convw_ref[0:1, :] * u_scr[pl.ds(halo - 2 + r.start, sub), :]
                + convw_ref[1:2, :] * u_scr[pl.ds(halo - 1 + r.start, sub), :]
                + convw_ref[2:3, :] * u)
        yconv_ref[r, :] = _group_norm_lanes(proj(n, 0) * conv, gconv_ref[...]).astype(BF16)
    utail_ref[...] = u_scr[halo + tail_start:halo + tail_start + halo, :]
    u_scr[0:halo, :] = u_scr[tm:tm + halo, :]

    tk = vt_ref.shape[2]
    pad_rows = HEAD_SLOT - HEAD_DIM
    ones_row = jnp.where(lax.broadcasted_iota(jnp.int32, (pad_rows, tk), 0) == 0, 1.0, 0.0).astype(BF16)
    for n, r in zip(n2, parts):
        q_ref[r, :] = (proj(n, 3) * (HEAD_DIM ** -0.5 * LOG2_E)).astype(BF16)
        vt = proj(n, 5).T
        for c in range(sub // tk):
            chunk = r.start // tk + c
            for h in range(n_heads):
                vt_ref[chunk, h * HEAD_SLOT:h * HEAD_SLOT + HEAD_DIM, :] = (
                    vt[h * HEAD_DIM:(h + 1) * HEAD_DIM, c * tk:(c + 1) * tk].astype(BF16))
                vt_ref[chunk, h * HEAD_SLOT + HEAD_DIM:(h + 1) * HEAD_SLOT, :] = ones_row

    for n, gate, r in zip(n2, gates, parts):
        k = proj(n, 4).astype(BF16)
        for pair in range(cw // HEAD_PAIR):
            base = pair * 2 * HEAD_PAIR
            kaug_ref[r, base:base + HEAD_PAIR] = k[:, pair * HEAD_PAIR:(pair + 1) * HEAD_PAIR]
            kaug_ref[r, base + HEAD_PAIR:base + 2 * HEAD_PAIR] = gate


def _pre_mixer_call(x2, g1, wgu, wdown, gm, win, wfg, bfp, convw, gconv, uinit, ginit, *, tm, tiles_per_seq,
                    tail_start, tk, n_heads):
    n_rows, d = x2.shape
    cw = gconv.shape[1]
    d_ff = wdown.shape[0]
    n_tiles = n_rows // tm
    halo = V7X_SUBLANES

    def resident(shape):
        return pl.BlockSpec(shape, lambda t: (0,) * len(shape), pipeline_mode=pl.Buffered(1))

    def rows(width):
        return pl.BlockSpec((tm, width), lambda t: (t, 0))

    body = functools.partial(_pre_mixer_body, tiles_per_seq=tiles_per_seq, tail_start=tail_start,
                             n_heads=n_heads)
    return pl.pallas_call(
        body,
        grid=(n_tiles,),
        in_specs=[rows(d), resident(g1.shape), resident(wgu.shape), resident(wdown.shape),
                  resident(gm.shape), resident(win.shape), resident(wfg.shape), resident(bfp.shape),
                  resident(convw.shape), resident(gconv.shape), resident(uinit.shape),
                  resident(ginit.shape)],
        out_specs=[rows(d), rows(cw), rows(2 * cw),
                   pl.BlockSpec((tm // tk, n_heads * HEAD_SLOT, tk), lambda t: (t, 0, 0)),
                   rows(cw),
                   pl.BlockSpec((halo, cw), lambda t: (t, 0)),
                   pl.BlockSpec((n_heads, V7X_LANES), lambda t: (t, 0))],
        out_shape=[jax.ShapeDtypeStruct((n_rows, d), F32),
                   jax.ShapeDtypeStruct((n_rows, cw), BF16),
                   jax.ShapeDtypeStruct((n_rows, 2 * cw), BF16),
                   jax.ShapeDtypeStruct((n_rows // tk, n_heads * HEAD_SLOT, tk), BF16),
                   jax.ShapeDtypeStruct((n_rows, cw), BF16),
                   jax.ShapeDtypeStruct((n_tiles * halo, cw), F32),
                   jax.ShapeDtypeStruct((n_tiles * n_heads, V7X_LANES), F32)],
        scratch_shapes=[pltpu.VMEM((tm, d_ff), BF16),
                        pltpu.VMEM((tm + halo, cw), F32),
                        pltpu.VMEM((n_heads, V7X_LANES), F32)],
        compiler_params=pltpu.CompilerParams(dimension_semantics=("arbitrary",),
                                             vmem_limit_bytes=_vmem_limit()),
        name="pre_mixer",
    )(x2, g1, wgu, wdown, gm, win, wfg, bfp, convw, gconv, uinit, ginit)


def _attention_body(q_ref, kaug_ref, vt_ref, kmeta_ref, vtmeta_ref, gattn_ref, y_ref,
                    qm_scr, s_scr, m_scr, acc_scr, out_scr, *, q_tiles):
    tq = q_ref.shape[0]
    tk = vt_ref.shape[2]
    n_heads = q_ref.shape[1] // HEAD_DIM
    neg_inf = float("-inf")
    pair_row = lax.broadcasted_iota(jnp.int32, (HEAD_PAIR, tq), 0)

    def build_query_operands(pair):
        qt = q_ref[:, pair * HEAD_PAIR:(pair + 1) * HEAD_PAIR].astype(F32).T.astype(BF16)
        for half in range(2):
            h = 2 * pair + half
            mine = (pair_row >= half * HEAD_DIM) & (pair_row < (half + 1) * HEAD_DIM)
            picks = (pair_row == h) | (pair_row == n_heads + h) | (pair_row == 2 * n_heads + h)
            qm_scr[h, 0:HEAD_PAIR, :] = jnp.where(mine, qt, jnp.zeros((), BF16))
            qm_scr[h, HEAD_PAIR:2 * HEAD_PAIR, :] = jnp.where(picks, -1.0, 0.0).astype(BF16)

    all_q = slice(0, tq)
    late_q = slice(tq - tk, tq)

    def qk(kaug_rows, h, lanes):
        width = 2 * HEAD_PAIR
        return _mxu_dot(kaug_rows[:, (h // 2) * width:(h // 2 + 1) * width], qm_scr[h, :, lanes])

    def key_rows(c):
        return kaug_ref.at[c * tk:(c + 1) * tk, :]

    def fold_head(h, s, vt_blk, mask, first, lanes):
        n = lanes.stop - lanes.start
        slot_rows = slice(h * HEAD_SLOT, (h + 1) * HEAD_SLOT)
        if mask is not None:
            masked = jnp.where(mask, s[:, 0:mask.shape[1]], neg_inf)
            s = masked if mask.shape[1] == n else jnp.concatenate([masked, s[:, mask.shape[1]:]], axis=1)
        c_max = jnp.max(s, axis=0, keepdims=True)
        if first:
            m_new = c_max
        else:
            m_old = m_scr[h:h + 1, lanes]
            m_new = jnp.maximum(m_old, c_max)
            alpha = jnp.exp2(m_old - m_new)
        p = jnp.exp2(s - m_new).astype(BF16)
        if vt_blk.shape[1] > p.shape[0]:
            p = jnp.concatenate([p, jnp.zeros((vt_blk.shape[1] - p.shape[0], n), BF16)], axis=0)
        m_scr[h:h + 1, lanes] = m_new
        pv = _mxu_dot(vt_blk[slot_rows, :], p)
        acc_scr[slot_rows, lanes] = pv if first else alpha * acc_scr[slot_rows, lanes] + pv

    def stage_and_fold(c_next, slot_next, fold_one):
        rows = key_rows(c_next)
        for h in range(n_heads):
            s_scr[slot_next, h] = qk(rows, h, all_q)
            fold_one(h)

    def tile_program(i):
        meta_rows = kmeta_ref.at[0:N_META, :]
        first_rows = key_rows(0)
        meta_scores = []
        for pair in range(n_heads // 2):
            build_query_operands(pair)
            for h in (2 * pair, 2 * pair + 1):
                meta_scores.append(qk(meta_rows, h, all_q))
                s_scr[0, h] = qk(first_rows, h, all_q)
        for h in range(n_heads):
            fold_head(h, meta_scores[h], vtmeta_ref, None, True, all_q)

        for c in range(2 * i):
            stage_and_fold(c + 1, (c + 1) % 2,
                           lambda h, c=c: fold_head(h, s_scr[c % 2, h], vt_ref.at[c], None, False, all_q))

        c = 2 * i
        last_rows = key_rows(c + 1)
        causal = (lax.broadcasted_iota(jnp.int32, (tk, tk), 0)
                  <= lax.broadcasted_iota(jnp.int32, (tk, tk), 1))
        late_scores = []
        for h in range(n_heads):
            late_scores.append(qk(last_rows, h, late_q))
            fold_head(h, s_scr[0, h], vt_ref.at[c], causal, False, all_q)
        for h in range(n_heads):
            fold_head(h, late_scores[h], vt_ref.at[c + 1], causal, False, late_q)

        for h in range(n_heads):
            base = h * HEAD_SLOT
            o = acc_scr[base:base + HEAD_DIM, :] / acc_scr[base + HEAD_DIM:base + HEAD_DIM + 1, :]
            out_scr[h * HEAD_DIM:(h + 1) * HEAD_DIM, :] = (
                o * lax.rsqrt(jnp.mean(o * o, axis=0, keepdims=True) + EPS))
        y_ref[...] = (out_scr[...].T * gattn_ref[...]).astype(BF16)

    for tile in range(q_tiles):
        pl.when(pl.program_id(1) == tile)(functools.partial(tile_program, tile))


def _attention_call(q, kaug, vt, kaug_meta, vt_meta, gattn, *, batch, seq, tq):
    cw = q.shape[1]
    tk = vt.shape[2]
    assert tq == 2 * tk, "the diagonal handling assumes two key chunks per query tile"
    chunks = seq // tk
    n_heads = cw // HEAD_DIM
    slots = vt.shape[1]
    vt4 = vt.reshape(batch, chunks, slots, tk)
    q_tiles = seq // tq

    def const(shape):
        return pl.BlockSpec(shape, lambda b, i: (0,) * len(shape))

    return pl.pallas_call(
        functools.partial(_attention_body, q_tiles=q_tiles),
        grid=(batch, q_tiles),
        in_specs=[pl.BlockSpec((tq, cw), lambda b, i: (b * q_tiles + i, 0)),
                  pl.BlockSpec((seq, 2 * cw), lambda b, i: (b, 0)),
                  pl.BlockSpec((None, chunks, slots, tk), lambda b, i: (b, 0, 0, 0)),
                  const(kaug_meta.shape), const(vt_meta.shape), const(gattn.shape)],
        out_specs=pl.BlockSpec((tq, cw), lambda b, i: (b * q_tiles + i, 0)),
        out_shape=jax.ShapeDtypeStruct(q.shape, BF16),
        scratch_shapes=[pltpu.VMEM((n_heads, 2 * HEAD_PAIR, tq), BF16),
                        pltpu.VMEM((2, n_heads, tk, tq), F32),
                        pltpu.VMEM((n_heads, tq), F32),
                        pltpu.VMEM((slots, tq), F32),
                        pltpu.VMEM((cw, tq), F32)],
        compiler_params=pltpu.CompilerParams(dimension_semantics=("parallel", "parallel"),
                                             vmem_limit_bytes=_vmem_limit()),
        name="fox_attention",
    )(q, kaug, vt4, kaug_meta, vt_meta, gattn)


def _post_mixer_body(h1_ref, yc_ref, ya_ref, wout_ref, g2_ref, wgu_ref, wdown_ref, gf_ref, out_ref, act_scr):
    cw = yc_ref.shape[1]
    parts = _row_parts(h1_ref.shape[0])
    h2 = [h1_ref[r, :]
          + _mxu_dot(yc_ref[r, :], wout_ref[0:cw, :])
          + _mxu_dot(ya_ref[r, :], wout_ref[cw:2 * cw, :]) for r in parts]
    n = [_rms_norm(h, g2_ref[...]).astype(BF16) for h in h2]
    ffn = _swiglu_ffn(n, parts, wgu_ref, wdown_ref, act_scr)
    for h, f, r in zip(h2, ffn, parts):
        out_ref[r, :] = _rms_norm(h + 0.5 * f, gf_ref[...])


def _post_mixer_call(h1, yconv, yattn, wout, g2, wgu, wdown, gf, *, tm):
    n_rows, d = h1.shape
    cw = yconv.shape[1]
    d_ff = wdown.shape[0]

    def resident(shape):
        return pl.BlockSpec(shape, lambda t: (0,) * len(shape), pipeline_mode=pl.Buffered(1))

    def rows(width):
        return pl.BlockSpec((tm, width), lambda t: (t, 0))

    return pl.pallas_call(
        _post_mixer_body,
        grid=(n_rows // tm,),
        in_specs=[rows(d), rows(cw), rows(cw), resident(wout.shape), resident(g2.shape),
                  resident(wgu.shape), resident(wdown.shape), resident(gf.shape)],
        out_specs=rows(d),
        out_shape=jax.ShapeDtypeStruct((n_rows, d), F32),
        scratch_shapes=[pltpu.VMEM((tm, d_ff), BF16)],
        compiler_params=pltpu.CompilerParams(dimension_semantics=("parallel",),
                                             vmem_limit_bytes=_vmem_limit()),
        name="post_mixer",
    )(h1, yconv, yattn, wout, g2, wgu, wdown, gf)


def _vmem_limit():
    return V7X_VMEM_BYTES * 7 // 8


def _layer(x, meta_tokens, ffn1_norm, ffn1_w_gu, ffn1_w_down, mix_norm, w_in, conv_w, b_f, out_norm_conv,
           out_norm_attn, w_out, ffn2_norm, ffn2_w_gu, ffn2_w_down, final_norm, *, tm, tq):
    batch, seq, d = x.shape
    assert ffn1_norm.shape[0] == 1, "one layer deep"
    assert seq % tm == 0 and seq % tq == 0 and tm % ATTN_KEY_CHUNK == 0
    cw = conv_w.shape[2]
    n_heads = b_f.shape[1]
    row = lambda a: a.reshape(1, -1).astype(F32)

    wgu1, wdown1 = ffn1_w_gu[0].astype(BF16), ffn1_w_down[0].astype(BF16)
    wgu2, wdown2 = ffn2_w_gu[0].astype(BF16), ffn2_w_down[0].astype(BF16)
    win = w_in[0, :, :6 * cw].astype(BF16)
    wfg = jnp.pad(w_in[0, :, 6 * cw:], ((0, 0), (0, V7X_LANES - n_heads))).astype(BF16)
    bfp = jnp.pad(row(b_f[0]), ((0, 0), (0, V7X_LANES - n_heads)))
    wout = w_out[0].astype(BF16)
    pre_args = (row(ffn1_norm[0]), wgu1, wdown1, row(mix_norm[0]), win, wfg, bfp, conv_w[0].astype(F32),
                row(out_norm_conv[0]))

    meta_rows = V7X_LANES
    x_meta = jnp.pad(meta_tokens.astype(F32), ((0, meta_rows - N_META), (0, 0)))
    _, _, kaug_meta, vt_meta, _, u_meta, g_meta = _pre_mixer_call(
        x_meta, *pre_args, jnp.zeros((V7X_SUBLANES, cw), F32), jnp.zeros((n_heads, V7X_LANES), F32),
        tm=meta_rows, tiles_per_seq=1, tail_start=N_META - V7X_SUBLANES, tk=meta_rows, n_heads=n_heads)

    h1, q, kaug, vt, yconv, _, _ = _pre_mixer_call(
        x.reshape(batch * seq, d), *pre_args, u_meta, g_meta,
        tm=tm, tiles_per_seq=seq // tm, tail_start=tm - V7X_SUBLANES, tk=ATTN_KEY_CHUNK, n_heads=n_heads)

    yattn = _attention_call(q, kaug, vt, kaug_meta, vt_meta.reshape(n_heads * HEAD_SLOT, meta_rows),
                            row(out_norm_attn[0]), batch=batch, seq=seq, tq=tq)

    out = _post_mixer_call(h1, yconv, yattn, wout, row(ffn2_norm[0]), wgu2, wdown2, row(final_norm),
                           tm=2 * tm)
    return out.reshape(batch, seq, d)


def kernel(x, meta_tokens, ffn1_norm, ffn1_w_gu, ffn1_w_down, mix_norm, w_in, conv_w, b_f, out_norm_conv,
           out_norm_attn, w_out, ffn2_norm, ffn2_w_gu, ffn2_w_down, final_norm):
    return _layer(x, meta_tokens, ffn1_norm, ffn1_w_gu, ffn1_w_down, mix_norm, w_in, conv_w, b_f,
                  out_norm_conv, out_norm_attn, w_out, ffn2_norm, ffn2_w_gu, ffn2_w_down, final_norm,
                  tm=TOKEN_TILE, tq=ATTN_TILE)
```

```python
import functools
import math

import jax
import jax.numpy as jnp
from jax import lax
from jax.experimental import pallas as pl
from jax.experimental.pallas import tpu as pltpu

EPS = 1e-6
LOG2_E = math.log2(math.e)
N_META = 16
HEAD_DIM = 64
CONV_K = 3

V7X_LANES = 128
V7X_SUBLANES = 8
V7X_MXU_DIM = 256
V7X_VMEM_BYTES = 64 * 1024 * 1024

TOKEN_TILE = 512
ROW_SUBTILE = 256
ATTN_KEY_CHUNK = V7X_MXU_DIM
ATTN_TILE = 2 * ATTN_KEY_CHUNK
FF_CHUNK = V7X_MXU_DIM
HEAD_PAIR = 2 * HEAD_DIM
V7X_BF16_ROW_PACK = 2 * V7X_SUBLANES
HEAD_SLOT = HEAD_DIM + V7X_BF16_ROW_PACK

F32 = jnp.float32
BF16 = jnp.bfloat16


def _mxu_dot(a, b):
    return jnp.dot(a, b, preferred_element_type=F32)


def _rms_norm(x, g):
    return x * lax.rsqrt(jnp.mean(x * x, axis=-1, keepdims=True) + EPS) * g


def _row_parts(tm):
    sub = min(tm, ROW_SUBTILE)
    return [slice(r, r + sub) for r in range(0, tm, sub)]


def _swiglu_ffn(ns, parts, wgu_ref, wdown_ref, act_scr):
    d_ff = wdown_ref.shape[0]
    for n, r in zip(ns, parts):
        for lo in range(0, d_ff, FF_CHUNK):
            g = _mxu_dot(n, wgu_ref[:, lo:lo + FF_CHUNK])
            u = _mxu_dot(n, wgu_ref[:, d_ff + lo:d_ff + lo + FF_CHUNK])
            act_scr[r, lo:lo + FF_CHUNK] = (g * jax.nn.sigmoid(g) * u).astype(BF16)
    return [_mxu_dot(act_scr[r, :], wdown_ref[...]) for r in parts]


def _group_norm_lanes(y, g):
    lane = lax.broadcasted_iota(jnp.int32, (y.shape[0], V7X_LANES), 1)
    low = lane < HEAD_DIM
    outs = []
    for c in range(0, y.shape[1], V7X_LANES):
        blk = y[:, c:c + V7X_LANES]
        sq = blk * blk
        ms_lo = jnp.sum(jnp.where(low, sq, 0.0), axis=-1, keepdims=True) * (1.0 / HEAD_DIM)
        ms_hi = jnp.sum(jnp.where(low, 0.0, sq), axis=-1, keepdims=True) * (1.0 / HEAD_DIM)
        inv = jnp.where(low, lax.rsqrt(ms_lo + EPS), lax.rsqrt(ms_hi + EPS))
        outs.append(blk * inv * g[:, c:c + V7X_LANES])
    return jnp.concatenate(outs, axis=-1)


def _log_sigmoid(x):
    z = -x
    return -(jnp.maximum(z, 0.0) + jnp.log1p(jnp.exp(-jnp.abs(z))))


def _cumsum_lanes(a):
    r, n = a.shape
    lane = lax.broadcasted_iota(jnp.int32, (r, V7X_LANES), 1)
    blocks = []
    offset = jnp.zeros((r, 1), F32)
    for c in range(0, n, V7X_LANES):
        blk = a[:, c:c + V7X_LANES]
        shift = 1
        while shift < V7X_LANES:
            blk = blk + jnp.where(lane >= shift, pltpu.roll(blk, shift, axis=1), 0.0)
            shift *= 2
        blk = blk + offset
        offset = blk[:, V7X_LANES - 1:V7X_LANES]
        blocks.append(blk)
    return jnp.concatenate(blocks, axis=-1)


def _gate_pieces(g, n_heads):
    lane = lax.broadcasted_iota(jnp.int32, g.shape, 1)
    hi = g.astype(BF16).astype(F32)
    rest = g - hi
    mid = rest.astype(BF16).astype(F32)
    low = rest - mid
    pieces = jnp.where(lane < n_heads, hi,
                       jnp.where(lane < 2 * n_heads, pltpu.roll(mid, n_heads, axis=1),
                                 jnp.where(lane < 3 * n_heads, pltpu.roll(low, 2 * n_heads, axis=1), 0.0)))
    return pieces.astype(BF16)


def _pre_mixer_body(x_ref, g1_ref, wgu_ref, wdown_ref, gm_ref, win_ref, wfg_ref, bf_ref, convw_ref,
                    gconv_ref, uinit_ref, ginit_ref,
                    h1_ref, q_ref, kaug_ref, vt_ref, yconv_ref, utail_ref, gtail_ref,
                    act_scr, u_scr, gcarry_scr, *, tiles_per_seq, tail_start, n_heads):
    tm = x_ref.shape[0]
    cw = q_ref.shape[1]
    halo = V7X_SUBLANES

    @pl.when(pl.program_id(0) % tiles_per_seq == 0)
    def _():
        u_scr[0:halo, :] = uinit_ref[...]
        gcarry_scr[...] = ginit_ref[...]

    parts = _row_parts(tm)
    sub = parts[0].stop
    xs = [x_ref[r, :] for r in parts]
    n1 = [_rms_norm(x, g1_ref[...]).astype(BF16) for x in xs]
    ffn = _swiglu_ffn(n1, parts, wgu_ref, wdown_ref, act_scr)
    n2 = []
    for x, f, r in zip(xs, ffn, parts):
        h1 = x + 0.5 * f
        h1_ref[r, :] = h1
        n2.append(_rms_norm(h1, gm_ref[...]).astype(BF16))

    def proj(n, j):
        return _mxu_dot(n, win_ref[:, j * cw:(j + 1) * cw])

    tail_col = tail_start + halo - 1
    g_carry = gcarry_scr[0:n_heads, 0:1]
    gates = []
    for n, r in zip(n2, parts):
        fg = _mxu_dot(n, wfg_ref[...]) + bf_ref[...]
        g_run = _cumsum_lanes(_log_sigmoid(fg).T[0:n_heads, :]) + g_carry
        g_carry = g_run[:, sub - 1:sub]
        if r.start <= tail_col < r.stop:
            gtail_ref[...] = jnp.broadcast_to(g_run[:, tail_col - r.start:tail_col - r.start + 1],
                                              gtail_ref.shape)
        g_rows = jnp.concatenate([g_run * LOG2_E, jnp.zeros((V7X_LANES - n_heads, sub), F32)], axis=0)
        gates.append(_gate_pieces(g_rows.T, n_heads))
    gcarry_scr[...] = jnp.broadcast_to(g_carry, gcarry_scr.shape)

    us = [proj(n, 1) * proj(n, 2) for n in n2]
    for u, r in zip(us, parts):
        u_scr[halo + r.start:halo + r.stop, :] = u
    for n, u, r in zip(n2, us, parts):
        conv = (convw_ref[0:1, :] * u_scr[pl.ds(halo - 2 + r.start, sub), :]
                + convw_ref[1:2, :] * u_scr[pl.ds(halo - 1 + r.start, sub), :]
                + convw_ref[2:3, :] * u)
        yconv_ref[r, :] = _group_norm_lanes(proj(n, 0) * conv, gconv_ref[...]).astype(BF16)
    utail_ref[...] = u_scr[halo + tail_start:halo + tail_start + halo, :]
    u_scr[0:halo, :] = u_scr[tm:tm + halo, :]

    tk = vt_ref.shape[2]
    pad_rows = HEAD_SLOT - HEAD_DIM
    ones_row = jnp.where(lax.broadcasted_iota(jnp.int32, (pad_rows, tk), 0) == 0, 1.0, 0.0).astype(BF16)
    for n, r in zip(n2, parts):
        q_ref[r, :] = (proj(n, 3) * (HEAD_DIM ** -0.5 * LOG2_E)).astype(BF16)
        vt = proj(n, 5).T
        for c in range(sub // tk):
            chunk = r.start // tk + c
            for h in range(n_heads):
                vt_ref[chunk, h * HEAD_SLOT:h * HEAD_SLOT + HEAD_DIM, :] = (
                    vt[h * HEAD_DIM:(h + 1) * HEAD_DIM, c * tk:(c + 1) * tk].astype(BF16))
                vt_ref[chunk, h * HEAD_SLOT + HEAD_DIM:(h + 1) * HEAD_SLOT, :] = ones_row

    for n, gate, r in zip(n2, gates, parts):
        k = proj(n, 4).astype(BF16)
        for pair in range(cw // HEAD_PAIR):
            base = pair * 2 * HEAD_PAIR
            kaug_ref[r, base:base + HEAD_PAIR] = k[:, pair * HEAD_PAIR:(pair + 1) * HEAD_PAIR]
            kaug_ref[r, base + HEAD_PAIR:base + 2 * HEAD_PAIR] = gate


def _pre_mixer_call(x2, g1, wgu, wdown, gm, win, wfg, bfp, convw, gconv, uinit, ginit, *, tm, tiles_per_seq,
                    tail_start, tk, n_heads):
    n_rows, d = x2.shape
    cw = gconv.shape[1]
    d_ff = wdown.shape[0]
    n_tiles = n_rows // tm
    halo = V7X_SUBLANES

    def resident(shape):
        return pl.BlockSpec(shape, lambda t: (0,) * len(shape), pipeline_mode=pl.Buffered(1))

    def rows(width):
        return pl.BlockSpec((tm, width), lambda t: (t, 0))

    body = functools.partial(_pre_mixer_body, tiles_per_seq=tiles_per_seq, tail_start=tail_start,
                             n_heads=n_heads)
    return pl.pallas_call(
        body,
        grid=(n_tiles,),
        in_specs=[rows(d), resident(g1.shape), resident(wgu.shape), resident(wdown.shape),
                  resident(gm.shape), resident(win.shape), resident(wfg.shape), resident(bfp.shape),
                  resident(convw.shape), resident(gconv.shape), resident(uinit.shape),
                  resident(ginit.shape)],
        out_specs=[rows(d), rows(cw), rows(2 * cw),
                   pl.BlockSpec((tm // tk, n_heads * HEAD_SLOT, tk), lambda t: (t, 0, 0)),
                   rows(cw),
                   pl.BlockSpec((halo, cw), lambda t: (t, 0)),
                   pl.BlockSpec((n_heads, V7X_LANES), lambda t: (t, 0))],
        out_shape=[jax.ShapeDtypeStruct((n_rows, d), F32),
                   jax.ShapeDtypeStruct((n_rows, cw), BF16),
                   jax.ShapeDtypeStruct((n_rows, 2 * cw), BF16),
                   jax.ShapeDtypeStruct((n_rows // tk, n_heads * HEAD_SLOT, tk), BF16),
                   jax.ShapeDtypeStruct((n_rows, cw), BF16),
                   jax.ShapeDtypeStruct((n_tiles * halo, cw), F32),
                   jax.ShapeDtypeStruct((n_tiles * n_heads, V7X_LANES), F32)],
        scratch_shapes=[pltpu.VMEM((tm, d_ff), BF16),
                        pltpu.VMEM((tm + halo, cw), F32),
                        pltpu.VMEM((n_heads, V7X_LANES), F32)],
        compiler_params=pltpu.CompilerParams(dimension_semantics=("arbitrary",),
                                             vmem_limit_bytes=_vmem_limit()),
        name="pre_mixer",
    )(x2, g1, wgu, wdown, gm, win, wfg, bfp, convw, gconv, uinit, ginit)


def _attention_body(q_ref, kaug_ref, vt_ref, kmeta_ref, vtmeta_ref, gattn_ref, y_ref,
                    qm_scr, s_scr, cmax_scr, m_scr, acc_scr, out_scr, *, q_tiles):
    tq = q_ref.shape[0]
    tk = vt_ref.shape[2]
    n_heads = q_ref.shape[1] // HEAD_DIM
    neg_inf = float("-inf")
    pair_row = lax.broadcasted_iota(jnp.int32, (HEAD_PAIR, tq), 0)

    def build_query_operands(pair):
        qt = q_ref[:, pair * HEAD_PAIR:(pair + 1) * HEAD_PAIR].astype(F32).T.astype(BF16)
        for half in range(2):
            h = 2 * pair + half
            mine = (pair_row >= half * HEAD_DIM) & (pair_row < (half + 1) * HEAD_DIM)
            picks = (pair_row == h) | (pair_row == n_heads + h) | (pair_row == 2 * n_heads + h)
            qm_scr[h, 0:HEAD_PAIR, :] = jnp.where(mine, qt, jnp.zeros((), BF16))
            qm_scr[h, HEAD_PAIR:2 * HEAD_PAIR, :] = jnp.where(picks, -1.0, 0.0).astype(BF16)

    all_q = slice(0, tq)
    late_q = slice(tq - tk, tq)

    def qk(kaug_rows, h, lanes):
        width = 2 * HEAD_PAIR
        return _mxu_dot(kaug_rows[:, (h // 2) * width:(h // 2 + 1) * width], qm_scr[h, :, lanes])

    def key_rows(c):
        return kaug_ref.at[c * tk:(c + 1) * tk, :]

    def masked_scores(kaug_rows, h, lanes, mask):
        s = qk(kaug_rows, h, lanes)
        if mask is not None:
            n = lanes.stop - lanes.start
            masked = jnp.where(mask, s[:, 0:mask.shape[1]], neg_inf)
            s = masked if mask.shape[1] == n else jnp.concatenate([masked, s[:, mask.shape[1]:]], axis=1)
        return s, jnp.max(s, axis=0, keepdims=True)

    def stage_head(c, slot, h, mask):
        s, c_max = masked_scores(key_rows(c), h, all_q, mask)
        s_scr[slot, h] = s
        cmax_scr[slot, h:h + 1, :] = c_max

    def fold_head(h, s, c_max, vt_blk, first, lanes):
        n = lanes.stop - lanes.start
        slot_rows = slice(h * HEAD_SLOT, (h + 1) * HEAD_SLOT)
        if first:
            m_new = c_max
        else:
            m_old = m_scr[h:h + 1, lanes]
            m_new = jnp.maximum(m_old, c_max)
            alpha = jnp.exp2(m_old - m_new)
        p = jnp.exp2(s - m_new).astype(BF16)
        if vt_blk.shape[1] > p.shape[0]:
            p = jnp.concatenate([p, jnp.zeros((vt_blk.shape[1] - p.shape[0], n), BF16)], axis=0)
        m_scr[h:h + 1, lanes] = m_new
        pv = _mxu_dot(vt_blk[slot_rows, :], p)
        acc_scr[slot_rows, lanes] = pv if first else alpha * acc_scr[slot_rows, lanes] + pv

    def fold_staged(c, h):
        slot = c % 2
        fold_head(h, s_scr[slot, h], cmax_scr[slot, h:h + 1, :], vt_ref.at[c], False, all_q)

    def tile_program(i):
        causal = (lax.broadcasted_iota(jnp.int32, (tk, tk), 0)
                  <= lax.broadcasted_iota(jnp.int32, (tk, tk), 1))
        diag = 2 * i
        mask_of = lambda c: causal if c == diag else None

        meta_rows = kmeta_ref.at[0:N_META, :]
        meta = []
        for pair in range(n_heads // 2):
            build_query_operands(pair)
            for h in (2 * pair, 2 * pair + 1):
                meta.append(masked_scores(meta_rows, h, all_q, None))
                stage_head(0, 0, h, mask_of(0))
        for h in range(n_heads):
            fold_head(h, meta[h][0], meta[h][1], vtmeta_ref, True, all_q)

        for c in range(diag):
            for h in range(n_heads):
                stage_head(c + 1, (c + 1) % 2, h, mask_of(c + 1))
                fold_staged(c, h)

        last_rows = key_rows(diag + 1)
        late = []
        for h in range(n_heads):
            late.append(masked_scores(last_rows, h, late_q, causal))
            fold_staged(diag, h)
        for h in range(n_heads):
            fold_head(h, late[h][0], late[h][1], vt_ref.at[diag + 1], False, late_q)

        for h in range(n_heads):
            base = h * HEAD_SLOT
            o = acc_scr[base:base + HEAD_DIM, :] / acc_scr[base + HEAD_DIM:base + HEAD_DIM + 1, :]
            out_scr[h * HEAD_DIM:(h + 1) * HEAD_DIM, :] = (
                o * lax.rsqrt(jnp.mean(o * o, axis=0, keepdims=True) + EPS))
        y_ref[...] = (out_scr[...].T * gattn_ref[...]).astype(BF16)

    for tile in range(q_tiles):
        pl.when(pl.program_id(1) == tile)(functools.partial(tile_program, tile))


def _attention_call(q, kaug, vt, kaug_meta, vt_meta, gattn, *, batch, seq, tq):
    cw = q.shape[1]
    tk = vt.shape[2]
    assert tq == 2 * tk, "the diagonal handling assumes two key chunks per query tile"
    chunks = seq // tk
    n_heads = cw // HEAD_DIM
    slots = vt.shape[1]
    vt4 = vt.reshape(batch, chunks, slots, tk)
    q_tiles = seq // tq

    def const(shape):
        return pl.BlockSpec(shape, lambda b, i: (0,) * len(shape))

    return pl.pallas_call(
        functools.partial(_attention_body, q_tiles=q_tiles),
        grid=(batch, q_tiles),
        in_specs=[pl.BlockSpec((tq, cw), lambda b, i: (b * q_tiles + i, 0)),
                  pl.BlockSpec((seq, 2 * cw), lambda b, i: (b, 0)),
                  pl.BlockSpec((None, chunks, slots, tk), lambda b, i: (b, 0, 0, 0)),
                  const(kaug_meta.shape), const(vt_meta.shape), const(gattn.shape)],
        out_specs=pl.BlockSpec((tq, cw), lambda b, i: (b * q_tiles + i, 0)),
        out_shape=jax.ShapeDtypeStruct(q.shape, BF16),
        scratch_shapes=[pltpu.VMEM((n_heads, 2 * HEAD_PAIR, tq), BF16),
                        pltpu.VMEM((2, n_heads, tk, tq), F32),
                        pltpu.VMEM((2, n_heads, tq), F32),
                        pltpu.VMEM((n_heads, tq), F32),
                        pltpu.VMEM((slots, tq), F32),
                        pltpu.VMEM((cw, tq), F32)],
        compiler_params=pltpu.CompilerParams(dimension_semantics=("parallel", "parallel"),
                                             vmem_limit_bytes=_vmem_limit()),
        name="fox_attention",
    )(q, kaug, vt4, kaug_meta, vt_meta, gattn)


def _post_mixer_body(h1_ref, yc_ref, ya_ref, wout_ref, g2_ref, wgu_ref, wdown_ref, gf_ref, out_ref, act_scr):
    cw = yc_ref.shape[1]
    parts = _row_parts(h1_ref.shape[0])
    h2 = [h1_ref[r, :]
          + _mxu_dot(yc_ref[r, :], wout_ref[0:cw, :])
          + _mxu_dot(ya_ref[r, :], wout_ref[cw:2 * cw, :]) for r in parts]
    n = [_rms_norm(h, g2_ref[...]).astype(BF16) for h in h2]
    ffn = _swiglu_ffn(n, parts, wgu_ref, wdown_ref, act_scr)
    for h, f, r in zip(h2, ffn, parts):
        out_ref[r, :] = _rms_norm(h + 0.5 * f, gf_ref[...])


def _post_mixer_call(h1, yconv, yattn, wout, g2, wgu, wdown, gf, *, tm):
    n_rows, d = h1.shape
    cw = yconv.shape[1]
    d_ff = wdown.shape[0]

    def resident(shape):
        return pl.BlockSpec(shape, lambda t: (0,) * len(shape), pipeline_mode=pl.Buffered(1))

    def rows(width):
        return pl.BlockSpec((tm, width), lambda t: (t, 0))

    return pl.pallas_call(
        _post_mixer_body,
        grid=(n_rows // tm,),
        in_specs=[rows(d), rows(cw), rows(cw), resident(wout.shape), resident(g2.shape),
                  resident(wgu.shape), resident(wdown.shape), resident(gf.shape)],
        out_specs=rows(d),
        out_shape=jax.ShapeDtypeStruct((n_rows, d), F32),
        scratch_shapes=[pltpu.VMEM((tm, d_ff), BF16)],
        compiler_params=pltpu.CompilerParams(dimension_semantics=("parallel",),
                                             vmem_limit_bytes=_vmem_limit()),
        name="post_mixer",
    )(h1, yconv, yattn, wout, g2, wgu, wdown, gf)


def _vmem_limit():
    return V7X_VMEM_BYTES * 7 // 8


def _layer(x, meta_tokens, ffn1_norm, ffn1_w_gu, ffn1_w_down, mix_norm, w_in, conv_w, b_f, out_norm_conv,
           out_norm_attn, w_out, ffn2_norm, ffn2_w_gu, ffn2_w_down, final_norm, *, tm, tq):
    batch, seq, d = x.shape
    assert ffn1_norm.shape[0] == 1, "one layer deep"
    assert seq % tm == 0 and seq % tq == 0 and tm % ATTN_KEY_CHUNK == 0
    cw = conv_w.shape[2]
    n_heads = b_f.shape[1]
    row = lambda a: a.reshape(1, -1).astype(F32)

    wgu1, wdown1 = ffn1_w_gu[0].astype(BF16), ffn1_w_down[0].astype(BF16)
    wgu2, wdown2 = ffn2_w_gu[0].astype(BF16), ffn2_w_down[0].astype(BF16)
    win = w_in[0, :, :6 * cw].astype(BF16)
    wfg = jnp.pad(w_in[0, :, 6 * cw:], ((0, 0), (0, V7X_LANES - n_heads))).astype(BF16)
    bfp = jnp.pad(row(b_f[0]), ((0, 0), (0, V7X_LANES - n_heads)))
    wout = w_out[0].astype(BF16)
    pre_args = (row(ffn1_norm[0]), wgu1, wdown1, row(mix_norm[0]), win, wfg, bfp, conv_w[0].astype(F32),
                row(out_norm_conv[0]))

    meta_rows = V7X_LANES
    x_meta = jnp.pad(meta_tokens.astype(F32), ((0, meta_rows - N_META), (0, 0)))
    _, _, kaug_meta, vt_meta, _, u_meta, g_meta = _pre_mixer_call(
        x_meta, *pre_args, jnp.zeros((V7X_SUBLANES, cw), F32), jnp.zeros((n_heads, V7X_LANES), F32),
        tm=meta_rows, tiles_per_seq=1, tail_start=N_META - V7X_SUBLANES, tk=meta_rows, n_heads=n_heads)

    h1, q, kaug, vt, yconv, _, _ = _pre_mixer_call(
        x.reshape(batch * seq, d), *pre_args, u_meta, g_meta,
        tm=tm, tiles_per_seq=seq // tm, tail_start=tm - V7X_SUBLANES, tk=ATTN_KEY_CHUNK, n_heads=n_heads)

    yattn = _attention_call(q, kaug, vt, kaug_meta, vt_meta.reshape(n_heads * HEAD_SLOT, meta_rows),
                            row(out_norm_attn[0]), batch=batch, seq=seq, tq=tq)

    out = _post_mixer_call(h1, yconv, yattn, wout, row(ffn2_norm[0]), wgu2, wdown2, row(final_norm),
                           tm=2 * tm)
    return out.reshape(batch, seq, d)


def kernel(x, meta_tokens, ffn1_norm, ffn1_w_gu, ffn1_w_down, mix_norm, w_in, conv_w, b_f, out_norm_conv,
           out_norm_attn, w_out, ffn2_norm, ffn2_w_gu, ffn2_w_down, final_norm):
    return _layer(x, meta_tokens, ffn1_norm, ffn1_w_gu, ffn1_w_down, mix_norm, w_in, conv_w, b_f,
                  out_norm_conv, out_norm_attn, w_out, ffn2_norm, ffn2_w_gu, ffn2_w_down, final_norm,
                  tm=TOKEN_TILE, tq=ATTN_TILE)
```

```python
import functools
import math

import jax
import jax.numpy as jnp
from jax import lax
from jax.experimental import pallas as pl
from jax.experimental.pallas import tpu as pltpu

EPS = 1e-6
LOG2_E = math.log2(math.e)
N_META = 16
HEAD_DIM = 64
CONV_K = 3

V7X_LANES = 128
V7X_SUBLANES = 8
V7X_MXU_DIM = 256
V7X_VMEM_BYTES = 64 * 1024 * 1024

TOKEN_TILE = 512
ROW_SUBTILE = 256
ATTN_KEY_CHUNK = V7X_MXU_DIM
ATTN_TILE = 4 * ATTN_KEY_CHUNK
FF_CHUNK = V7X_MXU_DIM
HEAD_PAIR = 2 * HEAD_DIM
V7X_BF16_ROW_PACK = 2 * V7X_SUBLANES
HEAD_SLOT = HEAD_DIM + V7X_BF16_ROW_PACK

F32 = jnp.float32
BF16 = jnp.bfloat16


def _mxu_dot(a, b):
    return jnp.dot(a, b, preferred_element_type=F32)


def _rms_norm(x, g):
    return x * lax.rsqrt(jnp.mean(x * x, axis=-1, keepdims=True) + EPS) * g


def _row_parts(tm):
    sub = min(tm, ROW_SUBTILE)
    return [slice(r, r + sub) for r in range(0, tm, sub)]


def _swiglu_ffn(ns, parts, wgu_ref, wdown_ref, act_scr):
    d_ff = wdown_ref.shape[0]
    for n, r in zip(ns, parts):
        for lo in range(0, d_ff, FF_CHUNK):
            g = _mxu_dot(n, wgu_ref[:, lo:lo + FF_CHUNK])
            u = _mxu_dot(n, wgu_ref[:, d_ff + lo:d_ff + lo + FF_CHUNK])
            act_scr[r, lo:lo + FF_CHUNK] = (g * jax.nn.sigmoid(g) * u).astype(BF16)
    return [_mxu_dot(act_scr[r, :], wdown_ref[...]) for r in parts]


def _group_norm_lanes(y, g):
    lane = lax.broadcasted_iota(jnp.int32, (y.shape[0], V7X_LANES), 1)
    low = lane < HEAD_DIM
    outs = []
    for c in range(0, y.shape[1], V7X_LANES):
        blk = y[:, c:c + V7X_LANES]
        sq = blk * blk
        ms_lo = jnp.sum(jnp.where(low, sq, 0.0), axis=-1, keepdims=True) * (1.0 / HEAD_DIM)
        ms_hi = jnp.sum(jnp.where(low, 0.0, sq), axis=-1, keepdims=True) * (1.0 / HEAD_DIM)
        inv = jnp.where(low, lax.rsqrt(ms_lo + EPS), lax.rsqrt(ms_hi + EPS))
        outs.append(blk * inv * g[:, c:c + V7X_LANES])
    return jnp.concatenate(outs, axis=-1)


def _log_sigmoid(x):
    z = -x
    return -(jnp.maximum(z, 0.0) + jnp.log1p(jnp.exp(-jnp.abs(z))))


def _cumsum_lanes(a):
    r, n = a.shape
    lane = lax.broadcasted_iota(jnp.int32, (r, V7X_LANES), 1)
    blocks = []
    offset = jnp.zeros((r, 1), F32)
    for c in range(0, n, V7X_LANES):
        blk = a[:, c:c + V7X_LANES]
        shift = 1
        while shift < V7X_LANES:
            blk = blk + jnp.where(lane >= shift, pltpu.roll(blk, shift, axis=1), 0.0)
            shift *= 2
        blk = blk + offset
        offset = blk[:, V7X_LANES - 1:V7X_LANES]
        blocks.append(blk)
    return jnp.concatenate(blocks, axis=-1)


def _gate_pieces(g, n_heads):
    lane = lax.broadcasted_iota(jnp.int32, g.shape, 1)
    hi = g.astype(BF16).astype(F32)
    rest = g - hi
    mid = rest.astype(BF16).astype(F32)
    low = rest - mid
    pieces = jnp.where(lane < n_heads, hi,
                       jnp.where(lane < 2 * n_heads, pltpu.roll(mid, n_heads, axis=1),
                                 jnp.where(lane < 3 * n_heads, pltpu.roll(low, 2 * n_heads, axis=1), 0.0)))
    return pieces.astype(BF16)


def _pre_mixer_body(x_ref, g1_ref, wgu_ref, wdown_ref, gm_ref, win_ref, wfg_ref, bf_ref, convw_ref,
                    gconv_ref, uinit_ref, ginit_ref,
                    h1_ref, q_ref, kaug_ref, vt_ref, yconv_ref, utail_ref, gtail_ref,
                    act_scr, u_scr, gcarry_scr, *, tiles_per_seq, tail_start, n_heads):
    tm = x_ref.shape[0]
    cw = q_ref.shape[1]
    halo = V7X_SUBLANES

    @pl.when(pl.program_id(0) % tiles_per_seq == 0)
    def _():
        u_scr[0:halo, :] = uinit_ref[...]
        gcarry_scr[...] = ginit_ref[...]

    parts = _row_parts(tm)
    sub = parts[0].stop
    xs = [x_ref[r, :] for r in parts]
    n1 = [_rms_norm(x, g1_ref[...]).astype(BF16) for x in xs]
    ffn = _swiglu_ffn(n1, parts, wgu_ref, wdown_ref, act_scr)
    n2 = []
    for x, f, r in zip(xs, ffn, parts):
        h1 = x + 0.5 * f
        h1_ref[r, :] = h1
        n2.append(_rms_norm(h1, gm_ref[...]).astype(BF16))

    def proj(n, j):
        return _mxu_dot(n, win_ref[:, j * cw:(j + 1) * cw])

    tail_col = tail_start + halo - 1
    g_carry = gcarry_scr[0:n_heads, 0:1]
    gates = []
    for n, r in zip(n2, parts):
        fg = _mxu_dot(n, wfg_ref[...]) + bf_ref[...]
        g_run = _cumsum_lanes(_log_sigmoid(fg).T[0:n_heads, :]) + g_carry
        g_carry = g_run[:, sub - 1:sub]
        if r.start <= tail_col < r.stop:
            gtail_ref[...] = jnp.broadcast_to(g_run[:, tail_col - r.start:tail_col - r.start + 1],
                                              gtail_ref.shape)
        g_rows = jnp.concatenate([g_run * LOG2_E, jnp.zeros((V7X_LANES - n_heads, sub), F32)], axis=0)
        gates.append(_gate_pieces(g_rows.T, n_heads))
    gcarry_scr[...] = jnp.broadcast_to(g_carry, gcarry_scr.shape)

    us = [proj(n, 1) * proj(n, 2) for n in n2]
    for u, r in zip(us, parts):
        u_scr[halo + r.start:halo + r.stop, :] = u
    for n, u, r in zip(n2, us, parts):
        conv = (convw_ref[0:1, :] * u_scr[pl.ds(halo - 2 + r.start, sub), :]
                + convw_ref[1:2, :] * u_scr[pl.ds(halo - 1 + r.start, sub), :]
                + convw_ref[2:3, :] * u)
        yconv_ref[r, :] = _group_norm_lanes(proj(n, 0) * conv, gconv_ref[...]).astype(BF16)
    utail_ref[...] = u_scr[halo + tail_start:halo + tail_start + halo, :]
    u_scr[0:halo, :] = u_scr[tm:tm + halo, :]

    tk = vt_ref.shape[2]
    pad_rows = HEAD_SLOT - HEAD_DIM
    ones_row = jnp.where(lax.broadcasted_iota(jnp.int32, (pad_rows, tk), 0) == 0, 1.0, 0.0).astype(BF16)
    for n, r in zip(n2, parts):
        q_ref[r, :] = (proj(n, 3) * (HEAD_DIM ** -0.5 * LOG2_E)).astype(BF16)
        vt = proj(n, 5).T
        for c in range(sub // tk):
            chunk = r.start // tk + c
            for h in range(n_heads):
                vt_ref[chunk, h * HEAD_SLOT:h * HEAD_SLOT + HEAD_DIM, :] = (
                    vt[h * HEAD_DIM:(h + 1) * HEAD_DIM, c * tk:(c + 1) * tk].astype(BF16))
                vt_ref[chunk, h * HEAD_SLOT + HEAD_DIM:(h + 1) * HEAD_SLOT, :] = ones_row

    for n, gate, r in zip(n2, gates, parts):
        k = proj(n, 4).astype(BF16)
        for pair in range(cw // HEAD_PAIR):
            base = pair * 2 * HEAD_PAIR
            kaug_ref[r, base:base + HEAD_PAIR] = k[:, pair * HEAD_PAIR:(pair + 1) * HEAD_PAIR]
            kaug_ref[r, base + HEAD_PAIR:base + 2 * HEAD_PAIR] = gate


def _pre_mixer_call(x2, g1, wgu, wdown, gm, win, wfg, bfp, convw, gconv, uinit, ginit, *, tm, tiles_per_seq,
                    tail_start, tk, n_heads):
    n_rows, d = x2.shape
    cw = gconv.shape[1]
    d_ff = wdown.shape[0]
    n_tiles = n_rows // tm
    halo = V7X_SUBLANES

    def resident(shape):
        return pl.BlockSpec(shape, lambda t: (0,) * len(shape), pipeline_mode=pl.Buffered(1))

    def rows(width):
        return pl.BlockSpec((tm, width), lambda t: (t, 0))

    body = functools.partial(_pre_mixer_body, tiles_per_seq=tiles_per_seq, tail_start=tail_start,
                             n_heads=n_heads)
    return pl.pallas_call(
        body,
        grid=(n_tiles,),
        in_specs=[rows(d), resident(g1.shape), resident(wgu.shape), resident(wdown.shape),
                  resident(gm.shape), resident(win.shape), resident(wfg.shape), resident(bfp.shape),
                  resident(convw.shape), resident(gconv.shape), resident(uinit.shape),
                  resident(ginit.shape)],
        out_specs=[rows(d), rows(cw), rows(2 * cw),
                   pl.BlockSpec((tm // tk, n_heads * HEAD_SLOT, tk), lambda t: (t, 0, 0)),
                   rows(cw),
                   pl.BlockSpec((halo, cw), lambda t: (t, 0)),
                   pl.BlockSpec((n_heads, V7X_LANES), lambda t: (t, 0))],
        out_shape=[jax.ShapeDtypeStruct((n_rows, d), F32),
                   jax.ShapeDtypeStruct((n_rows, cw), BF16),
                   jax.ShapeDtypeStruct((n_rows, 2 * cw), BF16),
                   jax.ShapeDtypeStruct((n_rows // tk, n_heads * HEAD_SLOT, tk), BF16),
                   jax.ShapeDtypeStruct((n_rows, cw), BF16),
                   jax.ShapeDtypeStruct((n_tiles * halo, cw), F32),
                   jax.ShapeDtypeStruct((n_tiles * n_heads, V7X_LANES), F32)],
        scratch_shapes=[pltpu.VMEM((tm, d_ff), BF16),
                        pltpu.VMEM((tm + halo, cw), F32),
                        pltpu.VMEM((n_heads, V7X_LANES), F32)],
        compiler_params=pltpu.CompilerParams(dimension_semantics=("arbitrary",),
                                             vmem_limit_bytes=_vmem_limit()),
        name="pre_mixer",
    )(x2, g1, wgu, wdown, gm, win, wfg, bfp, convw, gconv, uinit, ginit)


def _attention_body(q_ref, kaug_ref, vt_ref, kmeta_ref, vtmeta_ref, gattn_ref, y_ref,
                    qm_scr, s_scr, cmax_scr, m_scr, acc_scr, out_scr, *, q_tiles):
    tq = q_ref.shape[0]
    tk = vt_ref.shape[2]
    n_heads = q_ref.shape[1] // HEAD_DIM
    neg_inf = float("-inf")
    pair_row = lax.broadcasted_iota(jnp.int32, (HEAD_PAIR, tq), 0)

    def build_query_operands(pair):
        qt = q_ref[:, pair * HEAD_PAIR:(pair + 1) * HEAD_PAIR].astype(F32).T.astype(BF16)
        for half in range(2):
            h = 2 * pair + half
            mine = (pair_row >= half * HEAD_DIM) & (pair_row < (half + 1) * HEAD_DIM)
            picks = (pair_row == h) | (pair_row == n_heads + h) | (pair_row == 2 * n_heads + h)
            qm_scr[h, 0:HEAD_PAIR, :] = jnp.where(mine, qt, jnp.zeros((), BF16))
            qm_scr[h, HEAD_PAIR:2 * HEAD_PAIR, :] = jnp.where(picks, -1.0, 0.0).astype(BF16)

    all_q = slice(0, tq)

    def qk(kaug_rows, h, lanes):
        width = 2 * HEAD_PAIR
        return _mxu_dot(kaug_rows[:, (h // 2) * width:(h // 2 + 1) * width], qm_scr[h, :, lanes])

    def key_rows(c):
        return kaug_ref.at[c * tk:(c + 1) * tk, :]

    def masked_scores(kaug_rows, h, lanes, mask):
        s = qk(kaug_rows, h, lanes)
        if mask is not None:
            n = lanes.stop - lanes.start
            masked = jnp.where(mask, s[:, 0:mask.shape[1]], neg_inf)
            s = masked if mask.shape[1] == n else jnp.concatenate([masked, s[:, mask.shape[1]:]], axis=1)
        return s, jnp.max(s, axis=0, keepdims=True)


    def fold_head(h, s, c_max, vt_blk, first, lanes):
        n = lanes.stop - lanes.start
        slot_rows = slice(h * HEAD_SLOT, (h + 1) * HEAD_SLOT)
        if first:
            m_new = c_max
        else:
            m_old = m_scr[h:h + 1, lanes]
            m_new = jnp.maximum(m_old, c_max)
            alpha = jnp.exp2(m_old - m_new)
        p = jnp.exp2(s - m_new).astype(BF16)
        if vt_blk.shape[1] > p.shape[0]:
            p = jnp.concatenate([p, jnp.zeros((vt_blk.shape[1] - p.shape[0], n), BF16)], axis=0)
        m_scr[h:h + 1, lanes] = m_new
        pv = _mxu_dot(vt_blk[slot_rows, :], p)
        acc_scr[slot_rows, lanes] = pv if first else alpha * acc_scr[slot_rows, lanes] + pv

    def tile_program(i):
        causal = (lax.broadcasted_iota(jnp.int32, (tk, tk), 0)
                  <= lax.broadcasted_iota(jnp.int32, (tk, tk), 1))
        ratio = tq // tk
        first_diag = ratio * i
        n_chunks = first_diag + ratio

        def lanes_of(c):
            return all_q if c < first_diag else slice((c - first_diag) * tk, tq)

        def stage_head(c, h):
            lanes = lanes_of(c)
            s, c_max = masked_scores(key_rows(c), h, lanes, causal if c >= first_diag else None)
            s_scr[c % 2, h, :, lanes] = s
            cmax_scr[c % 2, h:h + 1, lanes] = c_max

        def fold_staged(c, h):
            lanes = lanes_of(c)
            fold_head(h, s_scr[c % 2, h, :, lanes], cmax_scr[c % 2, h:h + 1, lanes], vt_ref.at[c], False, lanes)

        meta_rows = kmeta_ref.at[0:N_META, :]
        meta = []
        for pair in range(n_heads // 2):
            build_query_operands(pair)
            for h in (2 * pair, 2 * pair + 1):
                meta.append(masked_scores(meta_rows, h, all_q, None))
                stage_head(0, h)
        for h in range(n_heads):
            fold_head(h, meta[h][0], meta[h][1], vtmeta_ref, True, all_q)

        for c in range(n_chunks):
            for h in range(n_heads):
                if c + 1 < n_chunks:
                    stage_head(c + 1, h)
                fold_staged(c, h)

        for h in range(n_heads):
            base = h * HEAD_SLOT
            o = acc_scr[base:base + HEAD_DIM, :] / acc_scr[base + HEAD_DIM:base + HEAD_DIM + 1, :]
            out_scr[h * HEAD_DIM:(h + 1) * HEAD_DIM, :] = (
                o * lax.rsqrt(jnp.mean(o * o, axis=0, keepdims=True) + EPS))
        y_ref[...] = (out_scr[...].T * gattn_ref[...]).astype(BF16)

    for tile in range(q_tiles):
        pl.when(pl.program_id(1) == tile)(functools.partial(tile_program, tile))


def _attention_call(q, kaug, vt, kaug_meta, vt_meta, gattn, *, batch, seq, tq):
    cw = q.shape[1]
    tk = vt.shape[2]
    assert tq % tk == 0 and seq % tq == 0
    chunks = seq // tk
    n_heads = cw // HEAD_DIM
    slots = vt.shape[1]
    vt4 = vt.reshape(batch, chunks, slots, tk)
    q_tiles = seq // tq

    def const(shape):
        return pl.BlockSpec(shape, lambda b, i: (0,) * len(shape))

    return pl.pallas_call(
        functools.partial(_attention_body, q_tiles=q_tiles),
        grid=(batch, q_tiles),
        in_specs=[pl.BlockSpec((tq, cw), lambda b, i: (b * q_tiles + i, 0)),
                  pl.BlockSpec((seq, 2 * cw), lambda b, i: (b, 0)),
                  pl.BlockSpec((None, chunks, slots, tk), lambda b, i: (b, 0, 0, 0)),
                  const(kaug_meta.shape), const(vt_meta.shape), const(gattn.shape)],
        out_specs=pl.BlockSpec((tq, cw), lambda b, i: (b * q_tiles + i, 0)),
        out_shape=jax.ShapeDtypeStruct(q.shape, BF16),
        scratch_shapes=[pltpu.VMEM((n_heads, 2 * HEAD_PAIR, tq), BF16),
                        pltpu.VMEM((2, n_heads, tk, tq), F32),
                        pltpu.VMEM((2, n_heads, tq), F32),
                        pltpu.VMEM((n_heads, tq), F32),
                        pltpu.VMEM((slots, tq), F32),
                        pltpu.VMEM((cw, tq), F32)],
        compiler_params=pltpu.CompilerParams(dimension_semantics=("parallel", "parallel"),
                                             vmem_limit_bytes=_vmem_limit()),
        name="fox_attention",
    )(q, kaug, vt4, kaug_meta, vt_meta, gattn)


def _post_mixer_body(h1_ref, yc_ref, ya_ref, wout_ref, g2_ref, wgu_ref, wdown_ref, gf_ref, out_ref, act_scr):
    cw = yc_ref.shape[1]
    parts = _row_parts(h1_ref.shape[0])
    h2 = [h1_ref[r, :]
          + _mxu_dot(yc_ref[r, :], wout_ref[0:cw, :])
          + _mxu_dot(ya_ref[r, :], wout_ref[cw:2 * cw, :]) for r in parts]
    n = [_rms_norm(h, g2_ref[...]).astype(BF16) for h in h2]
    ffn = _swiglu_ffn(n, parts, wgu_ref, wdown_ref, act_scr)
    for h, f, r in zip(h2, ffn, parts):
        out_ref[r, :] = _rms_norm(h + 0.5 * f, gf_ref[...])


def _post_mixer_call(h1, yconv, yattn, wout, g2, wgu, wdown, gf, *, tm):
    n_rows, d = h1.shape
    cw = yconv.shape[1]
    d_ff = wdown.shape[0]

    def resident(shape):
        return pl.BlockSpec(shape, lambda t: (0,) * len(shape), pipeline_mode=pl.Buffered(1))

    def rows(width):
        return pl.BlockSpec((tm, width), lambda t: (t, 0))

    return pl.pallas_call(
        _post_mixer_body,
        grid=(n_rows // tm,),
        in_specs=[rows(d), rows(cw), rows(cw), resident(wout.shape), resident(g2.shape),
                  resident(wgu.shape), resident(wdown.shape), resident(gf.shape)],
        out_specs=rows(d),
        out_shape=jax.ShapeDtypeStruct((n_rows, d), F32),
        scratch_shapes=[pltpu.VMEM((tm, d_ff), BF16)],
        compiler_params=pltpu.CompilerParams(dimension_semantics=("parallel",),
                                             vmem_limit_bytes=_vmem_limit()),
        name="post_mixer",
    )(h1, yconv, yattn, wout, g2, wgu, wdown, gf)


def _vmem_limit():
    return V7X_VMEM_BYTES * 7 // 8


def _layer(x, meta_tokens, ffn1_norm, ffn1_w_gu, ffn1_w_down, mix_norm, w_in, conv_w, b_f, out_norm_conv,
           out_norm_attn, w_out, ffn2_norm, ffn2_w_gu, ffn2_w_down, final_norm, *, tm, tq):
    batch, seq, d = x.shape
    assert ffn1_norm.shape[0] == 1, "one layer deep"
    assert seq % tm == 0 and seq % tq == 0 and tm % ATTN_KEY_CHUNK == 0
    cw = conv_w.shape[2]
    n_heads = b_f.shape[1]
    row = lambda a: a.reshape(1, -1).astype(F32)

    wgu1, wdown1 = ffn1_w_gu[0].astype(BF16), ffn1_w_down[0].astype(BF16)
    wgu2, wdown2 = ffn2_w_gu[0].astype(BF16), ffn2_w_down[0].astype(BF16)
    win = w_in[0, :, :6 * cw].astype(BF16)
    wfg = jnp.pad(w_in[0, :, 6 * cw:], ((0, 0), (0, V7X_LANES - n_heads))).astype(BF16)
    bfp = jnp.pad(row(b_f[0]), ((0, 0), (0, V7X_LANES - n_heads)))
    wout = w_out[0].astype(BF16)
    pre_args = (row(ffn1_norm[0]), wgu1, wdown1, row(mix_norm[0]), win, wfg, bfp, conv_w[0].astype(F32),
                row(out_norm_conv[0]))

    meta_rows = V7X_LANES
    x_meta = jnp.pad(meta_tokens.astype(F32), ((0, meta_rows - N_META), (0, 0)))
    _, _, kaug_meta, vt_meta, _, u_meta, g_meta = _pre_mixer_call(
        x_meta, *pre_args, jnp.zeros((V7X_SUBLANES, cw), F32), jnp.zeros((n_heads, V7X_LANES), F32),
        tm=meta_rows, tiles_per_seq=1, tail_start=N_META - V7X_SUBLANES, tk=meta_rows, n_heads=n_heads)

    h1, q, kaug, vt, yconv, _, _ = _pre_mixer_call(
        x.reshape(batch * seq, d), *pre_args, u_meta, g_meta,
        tm=tm, tiles_per_seq=seq // tm, tail_start=tm - V7X_SUBLANES, tk=ATTN_KEY_CHUNK, n_heads=n_heads)

    yattn = _attention_call(q, kaug, vt, kaug_meta, vt_meta.reshape(n_heads * HEAD_SLOT, meta_rows),
                            row(out_norm_attn[0]), batch=batch, seq=seq, tq=tq)

    out = _post_mixer_call(h1, yconv, yattn, wout, row(ffn2_norm[0]), wgu2, wdown2, row(final_norm),
                           tm=2 * tm)
    return out.reshape(batch, seq, d)


def kernel(x, meta_tokens, ffn1_norm, ffn1_w_gu, ffn1_w_down, mix_norm, w_in, conv_w, b_f, out_norm_conv,
           out_norm_attn, w_out, ffn2_norm, ffn2_w_gu, ffn2_w_down, final_norm):
    return _layer(x, meta_tokens, ffn1_norm, ffn1_w_gu, ffn1_w_down, mix_norm, w_in, conv_w, b_f,
                  out_norm_conv, out_norm_attn, w_out, ffn2_norm, ffn2_w_gu, ffn2_w_down, final_norm,
                  tm=TOKEN_TILE, tq=ATTN_TILE)
```

```python
import functools
import math

import jax
import jax.numpy as jnp
from jax import lax
from jax.experimental import pallas as pl
from jax.experimental.pallas import tpu as pltpu

EPS = 1e-6
LOG2_E = math.log2(math.e)
N_META = 16
HEAD_DIM = 64
CONV_K = 3

V7X_LANES = 128
V7X_SUBLANES = 8
V7X_MXU_DIM = 256
V7X_VMEM_BYTES = 64 * 1024 * 1024

TOKEN_TILE = 512
ROW_SUBTILE = 256
ATTN_KEY_CHUNK = V7X_MXU_DIM
ATTN_TILE = 4 * ATTN_KEY_CHUNK
FF_CHUNK = V7X_MXU_DIM
HEAD_PAIR = 2 * HEAD_DIM
V7X_BF16_ROW_PACK = 2 * V7X_SUBLANES
HEAD_SLOT = HEAD_DIM + V7X_BF16_ROW_PACK

F32 = jnp.float32
BF16 = jnp.bfloat16


def _mxu_dot(a, b):
    return jnp.dot(a, b, preferred_element_type=F32)


def _rms_norm(x, g):
    return x * lax.rsqrt(jnp.mean(x * x, axis=-1, keepdims=True) + EPS) * g


def _row_parts(tm):
    sub = min(tm, ROW_SUBTILE)
    return [slice(r, r + sub) for r in range(0, tm, sub)]


def _swiglu_ffn(ns, parts, wgu_ref, wdown_ref, act_scr):
    d_ff = wdown_ref.shape[0]
    for n, r in zip(ns, parts):
        for lo in range(0, d_ff, FF_CHUNK):
            g = _mxu_dot(n, wgu_ref[:, lo:lo + FF_CHUNK])
            u = _mxu_dot(n, wgu_ref[:, d_ff + lo:d_ff + lo + FF_CHUNK])
            act_scr[r, lo:lo + FF_CHUNK] = (g * jax.nn.sigmoid(g) * u).astype(BF16)
    return [_mxu_dot(act_scr[r, :], wdown_ref[...]) for r in parts]


def _group_norm_lanes(y, g):
    lane = lax.broadcasted_iota(jnp.int32, (y.shape[0], V7X_LANES), 1)
    low = lane < HEAD_DIM
    outs = []
    for c in range(0, y.shape[1], V7X_LANES):
        blk = y[:, c:c + V7X_LANES]
        sq = blk * blk
        ms_lo = jnp.sum(jnp.where(low, sq, 0.0), axis=-1, keepdims=True) * (1.0 / HEAD_DIM)
        ms_hi = jnp.sum(jnp.where(low, 0.0, sq), axis=-1, keepdims=True) * (1.0 / HEAD_DIM)
        inv = jnp.where(low, lax.rsqrt(ms_lo + EPS), lax.rsqrt(ms_hi + EPS))
        outs.append(blk * inv * g[:, c:c + V7X_LANES])
    return jnp.concatenate(outs, axis=-1)


def _log_sigmoid(x):
    z = -x
    return -(jnp.maximum(z, 0.0) + jnp.log1p(jnp.exp(-jnp.abs(z))))


def _cumsum_lanes(a):
    r, n = a.shape
    lane = lax.broadcasted_iota(jnp.int32, (r, V7X_LANES), 1)
    blocks = []
    offset = jnp.zeros((r, 1), F32)
    for c in range(0, n, V7X_LANES):
        blk = a[:, c:c + V7X_LANES]
        shift = 1
        while shift < V7X_LANES:
            blk = blk + jnp.where(lane >= shift, pltpu.roll(blk, shift, axis=1), 0.0)
            shift *= 2
        blk = blk + offset
        offset = blk[:, V7X_LANES - 1:V7X_LANES]
        blocks.append(blk)
    return jnp.concatenate(blocks, axis=-1)


GATE_PIECES = 3


def _gate_pieces(g, n_heads):
    lane = lax.broadcasted_iota(jnp.int32, g.shape, 1)
    hi = g.astype(BF16).astype(F32)
    rest = g - hi
    mid = rest.astype(BF16).astype(F32)
    pieces = (hi, mid, rest - mid)
    out = jnp.zeros_like(g)
    for j, piece in enumerate(pieces):
        lo = HEAD_DIM + j * n_heads
        out = jnp.where((lane >= lo) & (lane < lo + n_heads), pltpu.roll(piece, lo, axis=1), out)
    return out


def _key_block(k_pair, half, gate):
    lane = lax.broadcasted_iota(jnp.int32, k_pair.shape, 1)
    blk = pltpu.roll(k_pair, HEAD_DIM, axis=1) if half else k_pair
    return jnp.where(lane < HEAD_DIM, blk, gate).astype(BF16)


def _pre_mixer_body(x_ref, g1_ref, wgu_ref, wdown_ref, gm_ref, win_ref, wfg_ref, bf_ref, convw_ref,
                    gconv_ref, uinit_ref, ginit_ref,
                    h1_ref, q_ref, kaug_ref, vt_ref, yconv_ref, utail_ref, gtail_ref,
                    act_scr, u_scr, gcarry_scr, *, tiles_per_seq, tail_start, n_heads):
    tm = x_ref.shape[0]
    cw = q_ref.shape[1]
    halo = V7X_SUBLANES

    @pl.when(pl.program_id(0) % tiles_per_seq == 0)
    def _():
        u_scr[0:halo, :] = uinit_ref[...]
        gcarry_scr[...] = ginit_ref[...]

    parts = _row_parts(tm)
    sub = parts[0].stop
    xs = [x_ref[r, :] for r in parts]
    n1 = [_rms_norm(x, g1_ref[...]).astype(BF16) for x in xs]
    ffn = _swiglu_ffn(n1, parts, wgu_ref, wdown_ref, act_scr)
    n2 = []
    for x, f, r in zip(xs, ffn, parts):
        h1 = x + 0.5 * f
        h1_ref[r, :] = h1
        n2.append(_rms_norm(h1, gm_ref[...]).astype(BF16))

    def proj(n, j):
        return _mxu_dot(n, win_ref[:, j * cw:(j + 1) * cw])

    tail_col = tail_start + halo - 1
    g_carry = gcarry_scr[0:n_heads, 0:1]
    gates = []
    for n, r in zip(n2, parts):
        fg = _mxu_dot(n, wfg_ref[...]) + bf_ref[...]
        g_run = _cumsum_lanes(_log_sigmoid(fg).T[0:n_heads, :]) + g_carry
        g_carry = g_run[:, sub - 1:sub]
        if r.start <= tail_col < r.stop:
            gtail_ref[...] = jnp.broadcast_to(g_run[:, tail_col - r.start:tail_col - r.start + 1],
                                              gtail_ref.shape)
        g_rows = jnp.concatenate([g_run * LOG2_E, jnp.zeros((V7X_LANES - n_heads, sub), F32)], axis=0)
        gates.append(_gate_pieces(g_rows.T, n_heads))
    gcarry_scr[...] = jnp.broadcast_to(g_carry, gcarry_scr.shape)

    us = [proj(n, 1) * proj(n, 2) for n in n2]
    for u, r in zip(us, parts):
        u_scr[halo + r.start:halo + r.stop, :] = u
    for n, u, r in zip(n2, us, parts):
        conv = (convw_ref[0:1, :] * u_scr[pl.ds(halo - 2 + r.start, sub), :]
                + convw_ref[1:2, :] * u_scr[pl.ds(halo - 1 + r.start, sub), :]
                + convw_ref[2:3, :] * u)
        yconv_ref[r, :] = _group_norm_lanes(proj(n, 0) * conv, gconv_ref[...]).astype(BF16)
    utail_ref[...] = u_scr[halo + tail_start:halo + tail_start + halo, :]
    u_scr[0:halo, :] = u_scr[tm:tm + halo, :]

    tk = vt_ref.shape[2]
    pad_rows = HEAD_SLOT - HEAD_DIM
    ones_row = jnp.where(lax.broadcasted_iota(jnp.int32, (pad_rows, tk), 0) == 0, 1.0, 0.0).astype(BF16)
    for n, gate, r in zip(n2, gates, parts):
        k = proj(n, 4)
        for h in range(n_heads):
            pair, half = divmod(h, 2)
            kaug_ref[r, h * V7X_LANES:(h + 1) * V7X_LANES] = _key_block(
                k[:, pair * HEAD_PAIR:(pair + 1) * HEAD_PAIR], half, gate)
    for n, r in zip(n2, parts):
        vt = proj(n, 5).T
        for c in range(sub // tk):
            chunk = r.start // tk + c
            for h in range(n_heads):
                vt_ref[chunk, h * HEAD_SLOT:h * HEAD_SLOT + HEAD_DIM, :] = (
                    vt[h * HEAD_DIM:(h + 1) * HEAD_DIM, c * tk:(c + 1) * tk].astype(BF16))
                vt_ref[chunk, h * HEAD_SLOT + HEAD_DIM:(h + 1) * HEAD_SLOT, :] = ones_row
    for n, r in zip(n2, parts):
        q_ref[r, :] = (proj(n, 3) * (HEAD_DIM ** -0.5 * LOG2_E)).astype(BF16)


def _pre_mixer_call(x2, g1, wgu, wdown, gm, win, wfg, bfp, convw, gconv, uinit, ginit, *, tm, tiles_per_seq,
                    tail_start, tk, n_heads):
    n_rows, d = x2.shape
    cw = gconv.shape[1]
    d_ff = wdown.shape[0]
    n_tiles = n_rows // tm
    halo = V7X_SUBLANES

    def resident(shape):
        return pl.BlockSpec(shape, lambda t: (0,) * len(shape), pipeline_mode=pl.Buffered(1))

    def rows(width):
        return pl.BlockSpec((tm, width), lambda t: (t, 0))

    body = functools.partial(_pre_mixer_body, tiles_per_seq=tiles_per_seq, tail_start=tail_start,
                             n_heads=n_heads)
    return pl.pallas_call(
        body,
        grid=(n_tiles,),
        in_specs=[rows(d), resident(g1.shape), resident(wgu.shape), resident(wdown.shape),
                  resident(gm.shape), resident(win.shape), resident(wfg.shape), resident(bfp.shape),
                  resident(convw.shape), resident(gconv.shape), resident(uinit.shape),
                  resident(ginit.shape)],
        out_specs=[rows(d), rows(cw), rows(2 * cw),
                   pl.BlockSpec((tm // tk, n_heads * HEAD_SLOT, tk), lambda t: (t, 0, 0)),
                   rows(cw),
                   pl.BlockSpec((halo, cw), lambda t: (t, 0)),
                   pl.BlockSpec((n_heads, V7X_LANES), lambda t: (t, 0))],
        out_shape=[jax.ShapeDtypeStruct((n_rows, d), F32),
                   jax.ShapeDtypeStruct((n_rows, cw), BF16),
                   jax.ShapeDtypeStruct((n_rows, 2 * cw), BF16),
                   jax.ShapeDtypeStruct((n_rows // tk, n_heads * HEAD_SLOT, tk), BF16),
                   jax.ShapeDtypeStruct((n_rows, cw), BF16),
                   jax.ShapeDtypeStruct((n_tiles * halo, cw), F32),
                   jax.ShapeDtypeStruct((n_tiles * n_heads, V7X_LANES), F32)],
        scratch_shapes=[pltpu.VMEM((tm, d_ff), BF16),
                        pltpu.VMEM((tm + halo, cw), F32),
                        pltpu.VMEM((n_heads, V7X_LANES), F32)],
        compiler_params=pltpu.CompilerParams(dimension_semantics=("arbitrary",),
                                             vmem_limit_bytes=_vmem_limit()),
        name="pre_mixer",
    )(x2, g1, wgu, wdown, gm, win, wfg, bfp, convw, gconv, uinit, ginit)


def _attention_body(q_ref, kaug_ref, vt_ref, kmeta_ref, vtmeta_ref, gattn_ref, y_ref,
                    qm_scr, s_scr, cmax_scr, m_scr, acc_scr, out_scr, *, q_tiles):
    tq = q_ref.shape[0]
    tk = vt_ref.shape[2]
    n_heads = q_ref.shape[1] // HEAD_DIM
    neg_inf = float("-inf")
    piece_row = lax.broadcasted_iota(jnp.int32, (HEAD_DIM, tq), 0)

    def build_query_operands(pair):
        qt = q_ref[:, pair * HEAD_PAIR:(pair + 1) * HEAD_PAIR].astype(F32).T.astype(BF16)
        for half in range(2):
            h = 2 * pair + half
            mine = piece_row == h
            for j in range(1, GATE_PIECES):
                mine = mine | (piece_row == j * n_heads + h)
            qm_scr[h, 0:HEAD_DIM, :] = qt[half * HEAD_DIM:(half + 1) * HEAD_DIM, :]
            qm_scr[h, HEAD_DIM:2 * HEAD_DIM, :] = jnp.where(mine, -1.0, 0.0).astype(BF16)

    all_q = slice(0, tq)

    def qk(kaug_rows, h, lanes):
        return _mxu_dot(kaug_rows[:, h * V7X_LANES:(h + 1) * V7X_LANES], qm_scr[h, :, lanes])

    def key_rows(c):
        return kaug_ref.at[c * tk:(c + 1) * tk, :]

    def masked_scores(kaug_rows, h, lanes, mask):
        s = qk(kaug_rows, h, lanes)
        if mask is not None:
            n = lanes.stop - lanes.start
            masked = jnp.where(mask, s[:, 0:mask.shape[1]], neg_inf)
            s = masked if mask.shape[1] == n else jnp.concatenate([masked, s[:, mask.shape[1]:]], axis=1)
        return s, jnp.max(s, axis=0, keepdims=True)


    def fold_head(h, s, c_max, vt_blk, first, lanes):
        n = lanes.stop - lanes.start
        slot_rows = slice(h * HEAD_SLOT, (h + 1) * HEAD_SLOT)
        if first:
            m_new = c_max
        else:
            m_old = m_scr[h:h + 1, lanes]
            m_new = jnp.maximum(m_old, c_max)
            alpha = jnp.exp2(m_old - m_new)
        p = jnp.exp2(s - m_new).astype(BF16)
        if vt_blk.shape[1] > p.shape[0]:
            p = jnp.concatenate([p, jnp.zeros((vt_blk.shape[1] - p.shape[0], n), BF16)], axis=0)
        m_scr[h:h + 1, lanes] = m_new
        pv = _mxu_dot(vt_blk[slot_rows, :], p)
        acc_scr[slot_rows, lanes] = pv if first else alpha * acc_scr[slot_rows, lanes] + pv

    def tile_program(i):
        causal = (lax.broadcasted_iota(jnp.int32, (tk, tk), 0)
                  <= lax.broadcasted_iota(jnp.int32, (tk, tk), 1))
        ratio = tq // tk
        first_diag = ratio * i
        n_chunks = first_diag + ratio

        def lanes_of(c):
            return all_q if c < first_diag else slice((c - first_diag) * tk, tq)

        def stage_head(c, h):
            lanes = lanes_of(c)
            s, c_max = masked_scores(key_rows(c), h, lanes, causal if c >= first_diag else None)
            s_scr[c % 2, h, :, lanes] = s
            cmax_scr[c % 2, h:h + 1, lanes] = c_max

        def fold_staged(c, h):
            lanes = lanes_of(c)
            fold_head(h, s_scr[c % 2, h, :, lanes], cmax_scr[c % 2, h:h + 1, lanes], vt_ref.at[c], False, lanes)

        meta_rows = kmeta_ref.at[0:N_META, :]
        meta = []
        for pair in range(n_heads // 2):
            build_query_operands(pair)
            for h in (2 * pair, 2 * pair + 1):
                meta.append(masked_scores(meta_rows, h, all_q, None))
                stage_head(0, h)
        for h in range(n_heads):
            fold_head(h, meta[h][0], meta[h][1], vtmeta_ref, True, all_q)

        for c in range(n_chunks):
            for h in range(n_heads):
                if c + 1 < n_chunks:
                    stage_head(c + 1, h)
                fold_staged(c, h)

        for h in range(n_heads):
            base = h * HEAD_SLOT
            o = acc_scr[base:base + HEAD_DIM, :] / acc_scr[base + HEAD_DIM:base + HEAD_DIM + 1, :]
            out_scr[h * HEAD_DIM:(h + 1) * HEAD_DIM, :] = (
                o * lax.rsqrt(jnp.mean(o * o, axis=0, keepdims=True) + EPS))
        y_ref[...] = (out_scr[...].T * gattn_ref[...]).astype(BF16)

    for tile in range(q_tiles):
        pl.when(pl.program_id(1) == tile)(functools.partial(tile_program, tile))


def _attention_call(q, kaug, vt, kaug_meta, vt_meta, gattn, *, batch, seq, tq):
    cw = q.shape[1]
    tk = vt.shape[2]
    assert tq % tk == 0 and seq % tq == 0
    chunks = seq // tk
    n_heads = cw // HEAD_DIM
    slots = vt.shape[1]
    vt4 = vt.reshape(batch, chunks, slots, tk)
    q_tiles = seq // tq

    def const(shape):
        return pl.BlockSpec(shape, lambda b, i: (0,) * len(shape))

    return pl.pallas_call(
        functools.partial(_attention_body, q_tiles=q_tiles),
        grid=(batch, q_tiles),
        in_specs=[pl.BlockSpec((tq, cw), lambda b, i: (b * q_tiles + i, 0)),
                  pl.BlockSpec((seq, 2 * cw), lambda b, i: (b, 0)),
                  pl.BlockSpec((None, chunks, slots, tk), lambda b, i: (b, 0, 0, 0)),
                  const(kaug_meta.shape), const(vt_meta.shape), const(gattn.shape)],
        out_specs=pl.BlockSpec((tq, cw), lambda b, i: (b * q_tiles + i, 0)),
        out_shape=jax.ShapeDtypeStruct(q.shape, BF16),
        scratch_shapes=[pltpu.VMEM((n_heads, V7X_LANES, tq), BF16),
                        pltpu.VMEM((2, n_heads, tk, tq), F32),
                        pltpu.VMEM((2, n_heads, tq), F32),
                        pltpu.VMEM((n_heads, tq), F32),
                        pltpu.VMEM((slots, tq), F32),
                        pltpu.VMEM((cw, tq), F32)],
        compiler_params=pltpu.CompilerParams(dimension_semantics=("parallel", "parallel"),
                                             vmem_limit_bytes=_vmem_limit()),
        name="fox_attention",
    )(q, kaug, vt4, kaug_meta, vt_meta, gattn)


def _post_mixer_body(h1_ref, yc_ref, ya_ref, wout_ref, g2_ref, wgu_ref, wdown_ref, gf_ref, out_ref, act_scr):
    cw = yc_ref.shape[1]
    parts = _row_parts(h1_ref.shape[0])
    h2 = [h1_ref[r, :]
          + _mxu_dot(yc_ref[r, :], wout_ref[0:cw, :])
          + _mxu_dot(ya_ref[r, :], wout_ref[cw:2 * cw, :]) for r in parts]
    n = [_rms_norm(h, g2_ref[...]).astype(BF16) for h in h2]
    ffn = _swiglu_ffn(n, parts, wgu_ref, wdown_ref, act_scr)
    for h, f, r in zip(h2, ffn, parts):
        out_ref[r, :] = _rms_norm(h + 0.5 * f, gf_ref[...])


def _post_mixer_call(h1, yconv, yattn, wout, g2, wgu, wdown, gf, *, tm):
    n_rows, d = h1.shape
    cw = yconv.shape[1]
    d_ff = wdown.shape[0]

    def resident(shape):
        return pl.BlockSpec(shape, lambda t: (0,) * len(shape), pipeline_mode=pl.Buffered(1))

    def rows(width):
        return pl.BlockSpec((tm, width), lambda t: (t, 0))

    return pl.pallas_call(
        _post_mixer_body,
        grid=(n_rows // tm,),
        in_specs=[rows(d), rows(cw), rows(cw), resident(wout.shape), resident(g2.shape),
                  resident(wgu.shape), resident(wdown.shape), resident(gf.shape)],
        out_specs=rows(d),
        out_shape=jax.ShapeDtypeStruct((n_rows, d), F32),
        scratch_shapes=[pltpu.VMEM((tm, d_ff), BF16)],
        compiler_params=pltpu.CompilerParams(dimension_semantics=("parallel",),
                                             vmem_limit_bytes=_vmem_limit()),
        name="post_mixer",
    )(h1, yconv, yattn, wout, g2, wgu, wdown, gf)


def _vmem_limit():
    return V7X_VMEM_BYTES * 7 // 8


def _layer(x, meta_tokens, ffn1_norm, ffn1_w_gu, ffn1_w_down, mix_norm, w_in, conv_w, b_f, out_norm_conv,
           out_norm_attn, w_out, ffn2_norm, ffn2_w_gu, ffn2_w_down, final_norm, *, tm, tq):
    batch, seq, d = x.shape
    assert ffn1_norm.shape[0] == 1, "one layer deep"
    assert seq % tm == 0 and seq % tq == 0 and tm % ATTN_KEY_CHUNK == 0
    cw = conv_w.shape[2]
    n_heads = b_f.shape[1]
    row = lambda a: a.reshape(1, -1).astype(F32)

    wgu1, wdown1 = ffn1_w_gu[0].astype(BF16), ffn1_w_down[0].astype(BF16)
    wgu2, wdown2 = ffn2_w_gu[0].astype(BF16), ffn2_w_down[0].astype(BF16)
    win = w_in[0, :, :6 * cw].astype(BF16)
    wfg = jnp.pad(w_in[0, :, 6 * cw:], ((0, 0), (0, V7X_LANES - n_heads))).astype(BF16)
    bfp = jnp.pad(row(b_f[0]), ((0, 0), (0, V7X_LANES - n_heads)))
    wout = w_out[0].astype(BF16)
    pre_args = (row(ffn1_norm[0]), wgu1, wdown1, row(mix_norm[0]), win, wfg, bfp, conv_w[0].astype(F32),
                row(out_norm_conv[0]))

    meta_rows = V7X_LANES
    x_meta = jnp.pad(meta_tokens.astype(F32), ((0, meta_rows - N_META), (0, 0)))
    _, _, kaug_meta, vt_meta, _, u_meta, g_meta = _pre_mixer_call(
        x_meta, *pre_args, jnp.zeros((V7X_SUBLANES, cw), F32), jnp.zeros((n_heads, V7X_LANES), F32),
        tm=meta_rows, tiles_per_seq=1, tail_start=N_META - V7X_SUBLANES, tk=meta_rows, n_heads=n_heads)

    h1, q, kaug, vt, yconv, _, _ = _pre_mixer_call(
        x.reshape(batch * seq, d), *pre_args, u_meta, g_meta,
        tm=tm, tiles_per_seq=seq // tm, tail_start=tm - V7X_SUBLANES, tk=ATTN_KEY_CHUNK, n_heads=n_heads)

    yattn = _attention_call(q, kaug, vt, kaug_meta, vt_meta.reshape(n_heads * HEAD_SLOT, meta_rows),
                            row(out_norm_attn[0]), batch=batch, seq=seq, tq=tq)

    out = _post_mixer_call(h1, yconv, yattn, wout, row(ffn2_norm[0]), wgu2, wdown2, row(final_norm),
                           tm=2 * tm)
    return out.reshape(batch, seq, d)


def kernel(x, meta_tokens, ffn1_norm, ffn1_w_gu, ffn1_w_down, mix_norm, w_in, conv_w, b_f, out_norm_conv,
           out_norm_attn, w_out, ffn2_norm, ffn2_w_gu, ffn2_w_down, final_norm):
    return _layer(x, meta_tokens, ffn1_norm, ffn1_w_gu, ffn1_w_down, mix_norm, w_in, conv_w, b_f,
                  out_norm_conv, out_norm_attn, w_out, ffn2_norm, ffn2_w_gu, ffn2_w_down, final_norm,
                  tm=TOKEN_TILE, tq=ATTN_TILE)
```

```python
import functools
import math

import jax
import jax.numpy as jnp
from jax import lax
from jax.experimental import pallas as pl
from jax.experimental.pallas import tpu as pltpu

EPS = 1e-6
LOG2_E = math.log2(math.e)
N_META = 16
HEAD_DIM = 64
CONV_K = 3

V7X_LANES = 128
V7X_SUBLANES = 8
V7X_MXU_DIM = 256
V7X_VMEM_BYTES = 64 * 1024 * 1024

TOKEN_TILE = 512
ROW_SUBTILE = 256
ATTN_KEY_CHUNK = V7X_MXU_DIM
ATTN_TILE = 4 * ATTN_KEY_CHUNK
FF_CHUNK = V7X_MXU_DIM
HEAD_PAIR = 2 * HEAD_DIM
V7X_BF16_ROW_PACK = 2 * V7X_SUBLANES
HEAD_SLOT = HEAD_DIM + V7X_BF16_ROW_PACK

F32 = jnp.float32
BF16 = jnp.bfloat16


def _mxu_dot(a, b):
    return jnp.dot(a, b, preferred_element_type=F32)


def _rms_norm(x, g):
    return x * lax.rsqrt(jnp.mean(x * x, axis=-1, keepdims=True) + EPS) * g


def _row_parts(tm):
    sub = min(tm, ROW_SUBTILE)
    return [slice(r, r + sub) for r in range(0, tm, sub)]


def _swiglu_ffn(ns, parts, wgu_ref, wdown_ref, act_scr):
    d_ff = wdown_ref.shape[0]
    for n, r in zip(ns, parts):
        for lo in range(0, d_ff, FF_CHUNK):
            g = _mxu_dot(n, wgu_ref[:, lo:lo + FF_CHUNK])
            u = _mxu_dot(n, wgu_ref[:, d_ff + lo:d_ff + lo + FF_CHUNK])
            act_scr[r, lo:lo + FF_CHUNK] = (g * jax.nn.sigmoid(g) * u).astype(BF16)
    return [_mxu_dot(act_scr[r, :], wdown_ref[...]) for r in parts]


def _group_norm_lanes(y, g):
    lane = lax.broadcasted_iota(jnp.int32, (y.shape[0], V7X_LANES), 1)
    low = lane < HEAD_DIM
    outs = []
    for c in range(0, y.shape[1], V7X_LANES):
        blk = y[:, c:c + V7X_LANES]
        sq = blk * blk
        ms_lo = jnp.sum(jnp.where(low, sq, 0.0), axis=-1, keepdims=True) * (1.0 / HEAD_DIM)
        ms_hi = jnp.sum(jnp.where(low, 0.0, sq), axis=-1, keepdims=True) * (1.0 / HEAD_DIM)
        inv = jnp.where(low, lax.rsqrt(ms_lo + EPS), lax.rsqrt(ms_hi + EPS))
        outs.append(blk * inv * g[:, c:c + V7X_LANES])
    return jnp.concatenate(outs, axis=-1)


def _log_sigmoid(x):
    z = -x
    return -(jnp.maximum(z, 0.0) + jnp.log1p(jnp.exp(-jnp.abs(z))))


def _cumsum_lanes(a):
    r, n = a.shape
    lane = lax.broadcasted_iota(jnp.int32, (r, V7X_LANES), 1)
    blocks = []
    offset = jnp.zeros((r, 1), F32)
    for c in range(0, n, V7X_LANES):
        blk = a[:, c:c + V7X_LANES]
        shift = 1
        while shift < V7X_LANES:
            blk = blk + jnp.where(lane >= shift, pltpu.roll(blk, shift, axis=1), 0.0)
            shift *= 2
        blk = blk + offset
        offset = blk[:, V7X_LANES - 1:V7X_LANES]
        blocks.append(blk)
    return jnp.concatenate(blocks, axis=-1)


GATE_PIECES = 3


def _gate_pieces(g, n_heads):
    lane = lax.broadcasted_iota(jnp.int32, g.shape, 1)
    hi = g.astype(BF16).astype(F32)
    rest = g - hi
    mid = rest.astype(BF16).astype(F32)
    pieces = (hi, mid, rest - mid)
    out = jnp.zeros_like(g)
    for j, piece in enumerate(pieces):
        lo = HEAD_DIM + j * n_heads
        out = jnp.where((lane >= lo) & (lane < lo + n_heads), pltpu.roll(piece, lo, axis=1), out)
    return out


def _key_block(k_pair, half, gate):
    lane = lax.broadcasted_iota(jnp.int32, k_pair.shape, 1)
    blk = pltpu.roll(k_pair, HEAD_DIM, axis=1) if half else k_pair
    return jnp.where(lane < HEAD_DIM, blk, gate).astype(BF16)


def _pre_mixer_body(x_ref, g1_ref, wgu_ref, wdown_ref, gm_ref, win_ref, wfg_ref, bf_ref, convw_ref,
                    gconv_ref, uinit_ref, ginit_ref,
                    h1_ref, q_ref, kaug_ref, vt_ref, yconv_ref, utail_ref, gtail_ref,
                    act_scr, u_scr, gcarry_scr, *, tiles_per_seq, tail_start, n_heads):
    tm = x_ref.shape[0]
    cw = q_ref.shape[1]
    halo = V7X_SUBLANES

    @pl.when(pl.program_id(0) % tiles_per_seq == 0)
    def _():
        u_scr[0:halo, :] = uinit_ref[...]
        gcarry_scr[...] = ginit_ref[...]

    parts = _row_parts(tm)
    sub = parts[0].stop
    xs = [x_ref[r, :] for r in parts]
    n1 = [_rms_norm(x, g1_ref[...]).astype(BF16) for x in xs]
    ffn = _swiglu_ffn(n1, parts, wgu_ref, wdown_ref, act_scr)
    n2 = []
    for x, f, r in zip(xs, ffn, parts):
        h1 = x + 0.5 * f
        h1_ref[r, :] = h1
        n2.append(_rms_norm(h1, gm_ref[...]).astype(BF16))

    def proj(n, j):
        return _mxu_dot(n, win_ref[:, j * cw:(j + 1) * cw])

    tail_col = tail_start + halo - 1
    g_carry = gcarry_scr[0:n_heads, 0:1]
    gates = []
    for n, r in zip(n2, parts):
        fg = _mxu_dot(n, wfg_ref[...]) + bf_ref[...]
        g_run = _cumsum_lanes(_log_sigmoid(fg).T[0:n_heads, :]) + g_carry
        g_carry = g_run[:, sub - 1:sub]
        if r.start <= tail_col < r.stop:
            gtail_ref[...] = jnp.broadcast_to(g_run[:, tail_col - r.start:tail_col - r.start + 1],
                                              gtail_ref.shape)
        g_rows = jnp.concatenate([g_run * LOG2_E, jnp.zeros((V7X_LANES - n_heads, sub), F32)], axis=0)
        gates.append(_gate_pieces(g_rows.T, n_heads))
    gcarry_scr[...] = jnp.broadcast_to(g_carry, gcarry_scr.shape)

    us = [proj(n, 1) * proj(n, 2) for n in n2]
    for u, r in zip(us, parts):
        u_scr[halo + r.start:halo + r.stop, :] = u
    for n, u, r in zip(n2, us, parts):
        conv = (convw_ref[0:1, :] * u_scr[pl.ds(halo - 2 + r.start, sub), :]
                + convw_ref[1:2, :] * u_scr[pl.ds(halo - 1 + r.start, sub), :]
                + convw_ref[2:3, :] * u)
        yconv_ref[r, :] = _group_norm_lanes(proj(n, 0) * conv, gconv_ref[...]).astype(BF16)
    utail_ref[...] = u_scr[halo + tail_start:halo + tail_start + halo, :]
    u_scr[0:halo, :] = u_scr[tm:tm + halo, :]

    tk = vt_ref.shape[2]
    pad_rows = HEAD_SLOT - HEAD_DIM
    ones_row = jnp.where(lax.broadcasted_iota(jnp.int32, (pad_rows, tk), 0) == 0, 1.0, 0.0).astype(BF16)
    for n, gate, r in zip(n2, gates, parts):
        k = proj(n, 4)
        for h in range(n_heads):
            pair, half = divmod(h, 2)
            kaug_ref[r, h * V7X_LANES:(h + 1) * V7X_LANES] = _key_block(
                k[:, pair * HEAD_PAIR:(pair + 1) * HEAD_PAIR], half, gate)
    for n, r in zip(n2, parts):
        vt = proj(n, 5).T
        for c in range(sub // tk):
            chunk = r.start // tk + c
            for h in range(n_heads):
                vt_ref[chunk, h * HEAD_SLOT:h * HEAD_SLOT + HEAD_DIM, :] = (
                    vt[h * HEAD_DIM:(h + 1) * HEAD_DIM, c * tk:(c + 1) * tk].astype(BF16))
                vt_ref[chunk, h * HEAD_SLOT + HEAD_DIM:(h + 1) * HEAD_SLOT, :] = ones_row
    for n, r in zip(n2, parts):
        q_ref[r, :] = (proj(n, 3) * (HEAD_DIM ** -0.5 * LOG2_E)).astype(BF16)


def _pre_mixer_call(x2, g1, wgu, wdown, gm, win, wfg, bfp, convw, gconv, uinit, ginit, *, tm, tiles_per_seq,
                    tail_start, tk, n_heads):
    n_rows, d = x2.shape
    cw = gconv.shape[1]
    d_ff = wdown.shape[0]
    n_tiles = n_rows // tm
    halo = V7X_SUBLANES

    def resident(shape):
        return pl.BlockSpec(shape, lambda t: (0,) * len(shape), pipeline_mode=pl.Buffered(1))

    def rows(width):
        return pl.BlockSpec((tm, width), lambda t: (t, 0))

    body = functools.partial(_pre_mixer_body, tiles_per_seq=tiles_per_seq, tail_start=tail_start,
                             n_heads=n_heads)
    return pl.pallas_call(
        body,
        grid=(n_tiles,),
        in_specs=[rows(d), resident(g1.shape), resident(wgu.shape), resident(wdown.shape),
                  resident(gm.shape), resident((d, 6 * cw)), resident(wfg.shape), resident(bfp.shape),
                  resident(convw.shape), resident(gconv.shape), resident(uinit.shape),
                  resident(ginit.shape)],
        out_specs=[rows(d), rows(cw), rows(2 * cw),
                   pl.BlockSpec((tm // tk, n_heads * HEAD_SLOT, tk), lambda t: (t, 0, 0)),
                   rows(cw),
                   pl.BlockSpec((halo, cw), lambda t: (t, 0)),
                   pl.BlockSpec((n_heads, V7X_LANES), lambda t: (t, 0))],
        out_shape=[jax.ShapeDtypeStruct((n_rows, d), F32),
                   jax.ShapeDtypeStruct((n_rows, cw), BF16),
                   jax.ShapeDtypeStruct((n_rows, 2 * cw), BF16),
                   jax.ShapeDtypeStruct((n_rows // tk, n_heads * HEAD_SLOT, tk), BF16),
                   jax.ShapeDtypeStruct((n_rows, cw), BF16),
                   jax.ShapeDtypeStruct((n_tiles * halo, cw), F32),
                   jax.ShapeDtypeStruct((n_tiles * n_heads, V7X_LANES), F32)],
        scratch_shapes=[pltpu.VMEM((tm, d_ff), BF16),
                        pltpu.VMEM((tm + halo, cw), F32),
                        pltpu.VMEM((n_heads, V7X_LANES), F32)],
        compiler_params=pltpu.CompilerParams(dimension_semantics=("arbitrary",),
                                             vmem_limit_bytes=_vmem_limit()),
        name="pre_mixer",
    )(x2, g1, wgu, wdown, gm, win, wfg, bfp, convw, gconv, uinit, ginit)


def _attention_body(q_ref, kaug_ref, vt_ref, kmeta_ref, vtmeta_ref, gattn_ref, y_ref,
                    qm_scr, s_scr, cmax_scr, m_scr, acc_scr, out_scr, *, q_tiles):
    tq = q_ref.shape[0]
    tk = vt_ref.shape[2]
    n_heads = q_ref.shape[1] // HEAD_DIM
    neg_inf = float("-inf")
    piece_row = lax.broadcasted_iota(jnp.int32, (HEAD_DIM, tq), 0)

    def build_query_operands(pair):
        qt = q_ref[:, pair * HEAD_PAIR:(pair + 1) * HEAD_PAIR].astype(F32).T.astype(BF16)
        for half in range(2):
            h = 2 * pair + half
            mine = piece_row == h
            for j in range(1, GATE_PIECES):
                mine = mine | (piece_row == j * n_heads + h)
            qm_scr[h, 0:HEAD_DIM, :] = qt[half * HEAD_DIM:(half + 1) * HEAD_DIM, :]
            qm_scr[h, HEAD_DIM:2 * HEAD_DIM, :] = jnp.where(mine, -1.0, 0.0).astype(BF16)

    all_q = slice(0, tq)

    def qk(kaug_rows, h, lanes):
        return _mxu_dot(kaug_rows[:, h * V7X_LANES:(h + 1) * V7X_LANES], qm_scr[h, :, lanes])

    def key_rows(c):
        return kaug_ref.at[c * tk:(c + 1) * tk, :]

    def masked_scores(kaug_rows, h, lanes, mask):
        s = qk(kaug_rows, h, lanes)
        if mask is not None:
            n = lanes.stop - lanes.start
            masked = jnp.where(mask, s[:, 0:mask.shape[1]], neg_inf)
            s = masked if mask.shape[1] == n else jnp.concatenate([masked, s[:, mask.shape[1]:]], axis=1)
        return s, jnp.max(s, axis=0, keepdims=True)


    def fold_head(h, s, c_max, vt_blk, first, lanes):
        n = lanes.stop - lanes.start
        slot_rows = slice(h * HEAD_SLOT, (h + 1) * HEAD_SLOT)
        if first:
            m_new = c_max
        else:
            m_old = m_scr[h:h + 1, lanes]
            m_new = jnp.maximum(m_old, c_max)
            alpha = jnp.exp2(m_old - m_new)
        p = jnp.exp2(s - m_new).astype(BF16)
        if vt_blk.shape[1] > p.shape[0]:
            p = jnp.concatenate([p, jnp.zeros((vt_blk.shape[1] - p.shape[0], n), BF16)], axis=0)
        m_scr[h:h + 1, lanes] = m_new
        pv = _mxu_dot(vt_blk[slot_rows, :], p)
        acc_scr[slot_rows, lanes] = pv if first else alpha * acc_scr[slot_rows, lanes] + pv

    def tile_program(i):
        causal = (lax.broadcasted_iota(jnp.int32, (tk, tk), 0)
                  <= lax.broadcasted_iota(jnp.int32, (tk, tk), 1))
        ratio = tq // tk
        first_diag = ratio * i
        n_chunks = first_diag + ratio

        def lanes_of(c):
            return all_q if c < first_diag else slice((c - first_diag) * tk, tq)

        def stage_head(c, h):
            lanes = lanes_of(c)
            s, c_max = masked_scores(key_rows(c), h, lanes, causal if c >= first_diag else None)
            s_scr[c % 2, h, :, lanes] = s
            cmax_scr[c % 2, h:h + 1, lanes] = c_max

        def fold_staged(c, h):
            lanes = lanes_of(c)
            fold_head(h, s_scr[c % 2, h, :, lanes], cmax_scr[c % 2, h:h + 1, lanes], vt_ref.at[c], False, lanes)

        meta_rows = kmeta_ref.at[0:N_META, :]
        meta = []
        for pair in range(n_heads // 2):
            build_query_operands(pair)
            for h in (2 * pair, 2 * pair + 1):
                meta.append(masked_scores(meta_rows, h, all_q, None))
                stage_head(0, h)
        for h in range(n_heads):
            fold_head(h, meta[h][0], meta[h][1], vtmeta_ref, True, all_q)

        for c in range(n_chunks):
            for h in range(n_heads):
                if c + 1 < n_chunks:
                    stage_head(c + 1, h)
                fold_staged(c, h)

        for h in range(n_heads):
            base = h * HEAD_SLOT
            o = acc_scr[base:base + HEAD_DIM, :] / acc_scr[base + HEAD_DIM:base + HEAD_DIM + 1, :]
            out_scr[h * HEAD_DIM:(h + 1) * HEAD_DIM, :] = (
                o * lax.rsqrt(jnp.mean(o * o, axis=0, keepdims=True) + EPS))
        y_ref[...] = (out_scr[...].T * gattn_ref[...]).astype(BF16)

    for tile in range(q_tiles):
        pl.when(pl.program_id(1) == tile)(functools.partial(tile_program, tile))


def _attention_call(q, kaug, vt, kaug_meta, vt_meta, gattn, *, batch, seq, tq):
    cw = q.shape[1]
    tk = vt.shape[2]
    assert tq % tk == 0 and seq % tq == 0
    chunks = seq // tk
    n_heads = cw // HEAD_DIM
    slots = vt.shape[1]
    vt4 = vt.reshape(batch, chunks, slots, tk)
    q_tiles = seq // tq

    def const(shape):
        return pl.BlockSpec(shape, lambda b, i: (0,) * len(shape))

    return pl.pallas_call(
        functools.partial(_attention_body, q_tiles=q_tiles),
        grid=(batch, q_tiles),
        in_specs=[pl.BlockSpec((tq, cw), lambda b, i: (b * q_tiles + i, 0)),
                  pl.BlockSpec((seq, 2 * cw), lambda b, i: (b, 0)),
                  pl.BlockSpec((None, chunks, slots, tk), lambda b, i: (b, 0, 0, 0)),
                  const(kaug_meta.shape), const(vt_meta.shape), const(gattn.shape)],
        out_specs=pl.BlockSpec((tq, cw), lambda b, i: (b * q_tiles + i, 0)),
        out_shape=jax.ShapeDtypeStruct(q.shape, BF16),
        scratch_shapes=[pltpu.VMEM((n_heads, V7X_LANES, tq), BF16),
                        pltpu.VMEM((2, n_heads, tk, tq), F32),
                        pltpu.VMEM((2, n_heads, tq), F32),
                        pltpu.VMEM((n_heads, tq), F32),
                        pltpu.VMEM((slots, tq), F32),
                        pltpu.VMEM((cw, tq), F32)],
        compiler_params=pltpu.CompilerParams(dimension_semantics=("parallel", "parallel"),
                                             vmem_limit_bytes=_vmem_limit()),
        name="fox_attention",
    )(q, kaug, vt4, kaug_meta, vt_meta, gattn)


def _post_mixer_body(h1_ref, yc_ref, ya_ref, wout_ref, g2_ref, wgu_ref, wdown_ref, gf_ref, out_ref, act_scr):
    cw = yc_ref.shape[1]
    parts = _row_parts(h1_ref.shape[0])
    h2 = [h1_ref[r, :]
          + _mxu_dot(yc_ref[r, :], wout_ref[0:cw, :])
          + _mxu_dot(ya_ref[r, :], wout_ref[cw:2 * cw, :]) for r in parts]
    n = [_rms_norm(h, g2_ref[...]).astype(BF16) for h in h2]
    ffn = _swiglu_ffn(n, parts, wgu_ref, wdown_ref, act_scr)
    for h, f, r in zip(h2, ffn, parts):
        out_ref[r, :] = _rms_norm(h + 0.5 * f, gf_ref[...])


def _post_mixer_call(h1, yconv, yattn, wout, g2, wgu, wdown, gf, *, tm):
    n_rows, d = h1.shape
    cw = yconv.shape[1]
    d_ff = wdown.shape[0]

    def resident(shape):
        return pl.BlockSpec(shape, lambda t: (0,) * len(shape), pipeline_mode=pl.Buffered(1))

    def rows(width):
        return pl.BlockSpec((tm, width), lambda t: (t, 0))

    return pl.pallas_call(
        _post_mixer_body,
        grid=(n_rows // tm,),
        in_specs=[rows(d), rows(cw), rows(cw), resident(wout.shape), resident(g2.shape),
                  resident(wgu.shape), resident(wdown.shape), resident(gf.shape)],
        out_specs=rows(d),
        out_shape=jax.ShapeDtypeStruct((n_rows, d), F32),
        scratch_shapes=[pltpu.VMEM((tm, d_ff), BF16)],
        compiler_params=pltpu.CompilerParams(dimension_semantics=("parallel",),
                                             vmem_limit_bytes=_vmem_limit()),
        name="post_mixer",
    )(h1, yconv, yattn, wout, g2, wgu, wdown, gf)


def _vmem_limit():
    return V7X_VMEM_BYTES * 7 // 8


def _layer(x, meta_tokens, ffn1_norm, ffn1_w_gu, ffn1_w_down, mix_norm, w_in, conv_w, b_f, out_norm_conv,
           out_norm_attn, w_out, ffn2_norm, ffn2_w_gu, ffn2_w_down, final_norm, *, tm, tq):
    batch, seq, d = x.shape
    assert ffn1_norm.shape[0] == 1, "one layer deep"
    assert seq % tm == 0 and seq % tq == 0 and tm % ATTN_KEY_CHUNK == 0
    cw = conv_w.shape[2]
    n_heads = b_f.shape[1]
    row = lambda a: a.reshape(1, -1).astype(F32)

    wgu1, wdown1 = ffn1_w_gu[0].astype(BF16), ffn1_w_down[0].astype(BF16)
    wgu2, wdown2 = ffn2_w_gu[0].astype(BF16), ffn2_w_down[0].astype(BF16)
    win = w_in[0].astype(BF16)
    wfg = jnp.pad(w_in[0, :, 6 * cw:], ((0, 0), (0, V7X_LANES - n_heads))).astype(BF16)
    bfp = jnp.pad(row(b_f[0]), ((0, 0), (0, V7X_LANES - n_heads)))
    wout = w_out[0].astype(BF16)
    pre_args = (row(ffn1_norm[0]), wgu1, wdown1, row(mix_norm[0]), win, wfg, bfp, conv_w[0].astype(F32),
                row(out_norm_conv[0]))

    meta_rows = V7X_LANES
    x_meta = jnp.pad(meta_tokens.astype(F32), ((0, meta_rows - N_META), (0, 0)))
    _, _, kaug_meta, vt_meta, _, u_meta, g_meta = _pre_mixer_call(
        x_meta, *pre_args, jnp.zeros((V7X_SUBLANES, cw), F32), jnp.zeros((n_heads, V7X_LANES), F32),
        tm=meta_rows, tiles_per_seq=1, tail_start=N_META - V7X_SUBLANES, tk=meta_rows, n_heads=n_heads)

    h1, q, kaug, vt, yconv, _, _ = _pre_mixer_call(
        x.reshape(batch * seq, d), *pre_args, u_meta, g_meta,
        tm=tm, tiles_per_seq=seq // tm, tail_start=tm - V7X_SUBLANES, tk=ATTN_KEY_CHUNK, n_heads=n_heads)

    yattn = _attention_call(q, kaug, vt, kaug_meta, vt_meta.reshape(n_heads * HEAD_SLOT, meta_rows),
                            row(out_norm_attn[0]), batch=batch, seq=seq, tq=tq)

    out = _post_mixer_call(h1, yconv, yattn, wout, row(ffn2_norm[0]), wgu2, wdown2, row(final_norm),
                           tm=2 * tm)
    return out.reshape(batch, seq, d)


def kernel(x, meta_tokens, ffn1_norm, ffn1_w_gu, ffn1_w_down, mix_norm, w_in, conv_w, b_f, out_norm_conv,
           out_norm_attn, w_out, ffn2_norm, ffn2_w_gu, ffn2_w_down, final_norm):
    return _layer(x, meta_tokens, ffn1_norm, ffn1_w_gu, ffn1_w_down, mix_norm, w_in, conv_w, b_f,
                  out_norm_conv, out_norm_attn, w_out, ffn2_norm, ffn2_w_gu, ffn2_w_down, final_norm,
                  tm=TOKEN_TILE, tq=ATTN_TILE)
```

```python
import functools
import math

import jax
import jax.numpy as jnp
from jax import lax
from jax.experimental import pallas as pl
from jax.experimental.pallas import tpu as pltpu

EPS = 1e-6
LOG2_E = math.log2(math.e)
N_META = 16
HEAD_DIM = 64
CONV_K = 3
N_PROJ_GROUPS = 6

V7X_LANES = 128
V7X_SUBLANES = 8
V7X_MXU_DIM = 256
V7X_VMEM_BYTES = 64 * 1024 * 1024

TOKEN_TILE = 512
ROW_SUBTILE = 256
ATTN_KEY_CHUNK = V7X_MXU_DIM
ATTN_TILE = 4 * ATTN_KEY_CHUNK
FF_CHUNK = V7X_MXU_DIM
HEAD_PAIR = 2 * HEAD_DIM
V7X_BF16_ROW_PACK = 2 * V7X_SUBLANES
HEAD_SLOT = HEAD_DIM + V7X_BF16_ROW_PACK

F32 = jnp.float32
BF16 = jnp.bfloat16


def _mxu_dot(a, b):
    return jnp.dot(a, b, preferred_element_type=F32)


def _rms_norm(x, g):
    return x * lax.rsqrt(jnp.mean(x * x, axis=-1, keepdims=True) + EPS) * g


def _row_parts(tm):
    sub = min(tm, ROW_SUBTILE)
    return [slice(r, r + sub) for r in range(0, tm, sub)]


def _swiglu_ffn(ns, parts, wgu_ref, wdown_ref, act_scr):
    d_ff = wdown_ref.shape[0]
    for n, r in zip(ns, parts):
        for lo in range(0, d_ff, FF_CHUNK):
            g = _mxu_dot(n, wgu_ref[:, lo:lo + FF_CHUNK])
            u = _mxu_dot(n, wgu_ref[:, d_ff + lo:d_ff + lo + FF_CHUNK])
            act_scr[r, lo:lo + FF_CHUNK] = (g * jax.nn.sigmoid(g) * u).astype(BF16)
    return [_mxu_dot(act_scr[r, :], wdown_ref[...]) for r in parts]


def _group_norm_lanes(y, g):
    lane = lax.broadcasted_iota(jnp.int32, (y.shape[0], V7X_LANES), 1)
    low = lane < HEAD_DIM
    outs = []
    for c in range(0, y.shape[1], V7X_LANES):
        blk = y[:, c:c + V7X_LANES]
        sq = blk * blk
        ms_lo = jnp.sum(jnp.where(low, sq, 0.0), axis=-1, keepdims=True) * (1.0 / HEAD_DIM)
        ms_hi = jnp.sum(jnp.where(low, 0.0, sq), axis=-1, keepdims=True) * (1.0 / HEAD_DIM)
        inv = jnp.where(low, lax.rsqrt(ms_lo + EPS), lax.rsqrt(ms_hi + EPS))
        outs.append(blk * inv * g[:, c:c + V7X_LANES])
    return jnp.concatenate(outs, axis=-1)


def _log_sigmoid(x):
    z = -x
    return -(jnp.maximum(z, 0.0) + jnp.log1p(jnp.exp(-jnp.abs(z))))


def _cumsum_lanes(a):
    r, n = a.shape
    lane = lax.broadcasted_iota(jnp.int32, (r, V7X_LANES), 1)
    blocks = []
    offset = jnp.zeros((r, 1), F32)
    for c in range(0, n, V7X_LANES):
        blk = a[:, c:c + V7X_LANES]
        shift = 1
        while shift < V7X_LANES:
            blk = blk + jnp.where(lane >= shift, pltpu.roll(blk, shift, axis=1), 0.0)
            shift *= 2
        blk = blk + offset
        offset = blk[:, V7X_LANES - 1:V7X_LANES]
        blocks.append(blk)
    return jnp.concatenate(blocks, axis=-1)


GATE_PIECES = 3


def _gate_pieces(g, n_heads):
    lane = lax.broadcasted_iota(jnp.int32, g.shape, 1)
    hi = g.astype(BF16).astype(F32)
    rest = g - hi
    mid = rest.astype(BF16).astype(F32)
    pieces = (hi, mid, rest - mid)
    out = jnp.zeros_like(g)
    for j, piece in enumerate(pieces):
        lo = HEAD_DIM + j * n_heads
        out = jnp.where((lane >= lo) & (lane < lo + n_heads), pltpu.roll(piece, lo, axis=1), out)
    return out


def _key_block(k_pair, half, gate):
    lane = lax.broadcasted_iota(jnp.int32, k_pair.shape, 1)
    blk = pltpu.roll(k_pair, HEAD_DIM, axis=1) if half else k_pair
    return jnp.where(lane < HEAD_DIM, blk, gate).astype(BF16)


def _pre_mixer_body(x_ref, g1_ref, wgu_ref, wdown_ref, gm_ref, win_ref, wfg_ref, bf_ref, convw_ref,
                    gconv_ref, uinit_ref, ginit_ref,
                    h1_ref, q_ref, kaug_ref, vt_ref, yconv_ref, utail_ref, gtail_ref,
                    act_scr, u_scr, gcarry_scr, *, tiles_per_seq, tail_start, n_heads):
    tm = x_ref.shape[0]
    cw = q_ref.shape[1]
    halo = V7X_SUBLANES

    @pl.when(pl.program_id(0) % tiles_per_seq == 0)
    def _():
        u_scr[0:halo, :] = uinit_ref[...]
        gcarry_scr[...] = ginit_ref[...]

    parts = _row_parts(tm)
    sub = parts[0].stop
    xs = [x_ref[r, :] for r in parts]
    n1 = [_rms_norm(x, g1_ref[...]).astype(BF16) for x in xs]
    ffn = _swiglu_ffn(n1, parts, wgu_ref, wdown_ref, act_scr)
    n2 = []
    for x, f, r in zip(xs, ffn, parts):
        h1 = x + 0.5 * f
        h1_ref[r, :] = h1
        n2.append(_rms_norm(h1, gm_ref[...]).astype(BF16))

    def proj(n, j):
        return _mxu_dot(n, win_ref[:, j * cw:(j + 1) * cw])

    tail_col = tail_start + halo - 1
    g_carry = gcarry_scr[0:n_heads, 0:1]
    gates = []
    for n, r in zip(n2, parts):
        fg = _mxu_dot(n, wfg_ref[...]) + bf_ref[...]
        g_run = _cumsum_lanes(_log_sigmoid(fg).T[0:n_heads, :]) + g_carry
        g_carry = g_run[:, sub - 1:sub]
        if r.start <= tail_col < r.stop:
            gtail_ref[...] = jnp.broadcast_to(g_run[:, tail_col - r.start:tail_col - r.start + 1],
                                              gtail_ref.shape)
        g_rows = jnp.concatenate([g_run * LOG2_E, jnp.zeros((V7X_LANES - n_heads, sub), F32)], axis=0)
        gates.append(_gate_pieces(g_rows.T, n_heads))
    gcarry_scr[...] = jnp.broadcast_to(g_carry, gcarry_scr.shape)

    us = [proj(n, 1) * proj(n, 2) for n in n2]
    for u, r in zip(us, parts):
        u_scr[halo + r.start:halo + r.stop, :] = u
    for n, u, r in zip(n2, us, parts):
        conv = (convw_ref[0:1, :] * u_scr[pl.ds(halo - 2 + r.start, sub), :]
                + convw_ref[1:2, :] * u_scr[pl.ds(halo - 1 + r.start, sub), :]
                + convw_ref[2:3, :] * u)
        yconv_ref[r, :] = _group_norm_lanes(proj(n, 0) * conv, gconv_ref[...]).astype(BF16)
    utail_ref[...] = u_scr[halo + tail_start:halo + tail_start + halo, :]
    u_scr[0:halo, :] = u_scr[tm:tm + halo, :]

    for n, gate, r in zip(n2, gates, parts):
        k = proj(n, 4)
        for h in range(n_heads):
            pair, half = divmod(h, 2)
            kaug_ref[r, h * V7X_LANES:(h + 1) * V7X_LANES] = _key_block(
                k[:, pair * HEAD_PAIR:(pair + 1) * HEAD_PAIR], half, gate)
    tk = vt_ref.shape[2]
    pad_rows = HEAD_SLOT - HEAD_DIM
    ones_row = jnp.where(lax.broadcasted_iota(jnp.int32, (pad_rows, tk), 0) == 0, 1.0, 0.0).astype(BF16)
    for n, r in zip(n2, parts):
        vt = proj(n, 5).T
        for c in range(sub // tk):
            chunk = r.start // tk + c
            for h in range(n_heads):
                vt_ref[chunk, h * HEAD_SLOT:h * HEAD_SLOT + HEAD_DIM, :] = (
                    vt[h * HEAD_DIM:(h + 1) * HEAD_DIM, c * tk:(c + 1) * tk].astype(BF16))
                vt_ref[chunk, h * HEAD_SLOT + HEAD_DIM:(h + 1) * HEAD_SLOT, :] = ones_row
    for n, r in zip(n2, parts):
        q_ref[r, :] = (proj(n, 3) * (HEAD_DIM ** -0.5 * LOG2_E)).astype(BF16)


def _pre_mixer_call(x2, g1, wgu, wdown, gm, win, wfg, bfp, convw, gconv, uinit, ginit, *, tm, tiles_per_seq,
                    tail_start, tk, n_heads):
    n_rows, d = x2.shape
    cw = gconv.shape[1]
    d_ff = wdown.shape[0]
    n_tiles = n_rows // tm
    halo = V7X_SUBLANES

    def resident(shape):
        return pl.BlockSpec(shape, lambda t: (0,) * len(shape), pipeline_mode=pl.Buffered(1))

    def rows(width):
        return pl.BlockSpec((tm, width), lambda t: (t, 0))

    body = functools.partial(_pre_mixer_body, tiles_per_seq=tiles_per_seq, tail_start=tail_start,
                             n_heads=n_heads)
    return pl.pallas_call(
        body,
        grid=(n_tiles,),
        in_specs=[rows(d), resident(g1.shape), resident(wgu.shape), resident(wdown.shape),
                  resident(gm.shape), resident((d, N_PROJ_GROUPS * cw)), resident(wfg.shape),
                  resident(bfp.shape),
                  resident(convw.shape), resident(gconv.shape), resident(uinit.shape),
                  resident(ginit.shape)],
        out_specs=[rows(d), rows(cw), rows(2 * cw),
                   pl.BlockSpec((tm // tk, n_heads * HEAD_SLOT, tk), lambda t: (t, 0, 0)),
                   rows(cw),
                   pl.BlockSpec((halo, cw), lambda t: (t, 0)),
                   pl.BlockSpec((n_heads, V7X_LANES), lambda t: (t, 0))],
        out_shape=[jax.ShapeDtypeStruct((n_rows, d), F32),
                   jax.ShapeDtypeStruct((n_rows, cw), BF16),
                   jax.ShapeDtypeStruct((n_rows, 2 * cw), BF16),
                   jax.ShapeDtypeStruct((n_rows // tk, n_heads * HEAD_SLOT, tk), BF16),
                   jax.ShapeDtypeStruct((n_rows, cw), BF16),
                   jax.ShapeDtypeStruct((n_tiles * halo, cw), F32),
                   jax.ShapeDtypeStruct((n_tiles * n_heads, V7X_LANES), F32)],
        scratch_shapes=[pltpu.VMEM((tm, d_ff), BF16),
                        pltpu.VMEM((tm + halo, cw), F32),
                        pltpu.VMEM((n_heads, V7X_LANES), F32)],
        compiler_params=pltpu.CompilerParams(dimension_semantics=("arbitrary",),
                                             vmem_limit_bytes=_vmem_limit()),
        name="pre_mixer",
    )(x2, g1, wgu, wdown, gm, win, wfg, bfp, convw, gconv, uinit, ginit)


def _attention_body(q_ref, kaug_ref, vt_ref, kmeta_ref, vtmeta_ref, gattn_ref, y_ref,
                    qm_scr, s_scr, cmax_scr, m_scr, acc_scr, out_scr, *, q_tiles):
    tq = q_ref.shape[0]
    tk = vt_ref.shape[2]
    n_heads = q_ref.shape[1] // HEAD_DIM
    neg_inf = float("-inf")
    piece_row = lax.broadcasted_iota(jnp.int32, (HEAD_DIM, tq), 0)

    def build_query_operands(pair):
        qt = q_ref[:, pair * HEAD_PAIR:(pair + 1) * HEAD_PAIR].astype(F32).T.astype(BF16)
        for half in range(2):
            h = 2 * pair + half
            mine = piece_row == h
            for j in range(1, GATE_PIECES):
                mine = mine | (piece_row == j * n_heads + h)
            qm_scr[h, 0:HEAD_DIM, :] = qt[half * HEAD_DIM:(half + 1) * HEAD_DIM, :]
            qm_scr[h, HEAD_DIM:2 * HEAD_DIM, :] = jnp.where(mine, -1.0, 0.0).astype(BF16)

    all_q = slice(0, tq)

    def qk(kaug_rows, h, lanes):
        return _mxu_dot(kaug_rows[:, h * V7X_LANES:(h + 1) * V7X_LANES], qm_scr[h, :, lanes])

    def key_rows(c):
        return kaug_ref.at[c * tk:(c + 1) * tk, :]

    def masked_scores(kaug_rows, h, lanes, mask):
        s = qk(kaug_rows, h, lanes)
        if mask is not None:
            n = lanes.stop - lanes.start
            masked = jnp.where(mask, s[:, 0:mask.shape[1]], neg_inf)
            s = masked if mask.shape[1] == n else jnp.concatenate([masked, s[:, mask.shape[1]:]], axis=1)
        return s, jnp.max(s, axis=0, keepdims=True)


    def fold_head(h, s, c_max, vt_blk, first, lanes):
        n = lanes.stop - lanes.start
        slot_rows = slice(h * HEAD_SLOT, (h + 1) * HEAD_SLOT)
        if first:
            m_new = c_max
        else:
            m_old = m_scr[h:h + 1, lanes]
            m_new = jnp.maximum(m_old, c_max)
            alpha = jnp.exp2(m_old - m_new)
        p = jnp.exp2(s - m_new).astype(BF16)
        if vt_blk.shape[1] > p.shape[0]:
            p = jnp.concatenate([p, jnp.zeros((vt_blk.shape[1] - p.shape[0], n), BF16)], axis=0)
        m_scr[h:h + 1, lanes] = m_new
        pv = _mxu_dot(vt_blk[slot_rows, :], p)
        acc_scr[slot_rows, lanes] = pv if first else alpha * acc_scr[slot_rows, lanes] + pv

    def tile_program(i):
        causal = (lax.broadcasted_iota(jnp.int32, (tk, tk), 0)
                  <= lax.broadcasted_iota(jnp.int32, (tk, tk), 1))
        ratio = tq // tk
        first_diag = ratio * i
        n_chunks = first_diag + ratio

        def lanes_of(c):
            return all_q if c < first_diag else slice((c - first_diag) * tk, tq)

        def stage_head(c, h):
            lanes = lanes_of(c)
            s, c_max = masked_scores(key_rows(c), h, lanes, causal if c >= first_diag else None)
            s_scr[c % 2, h, :, lanes] = s
            cmax_scr[c % 2, h:h + 1, lanes] = c_max

        def fold_staged(c, h):
            lanes = lanes_of(c)
            fold_head(h, s_scr[c % 2, h, :, lanes], cmax_scr[c % 2, h:h + 1, lanes], vt_ref.at[c], False, lanes)

        meta_rows = kmeta_ref.at[0:N_META, :]
        meta = []
        for pair in range(n_heads // 2):
            build_query_operands(pair)
            for h in (2 * pair, 2 * pair + 1):
                meta.append(masked_scores(meta_rows, h, all_q, None))
                stage_head(0, h)
        for h in range(n_heads):
            fold_head(h, meta[h][0], meta[h][1], vtmeta_ref, True, all_q)

        for c in range(n_chunks):
            for h in range(n_heads):
                if c + 1 < n_chunks:
                    stage_head(c + 1, h)
                fold_staged(c, h)

        for h in range(n_heads):
            base = h * HEAD_SLOT
            o = acc_scr[base:base + HEAD_DIM, :] / acc_scr[base + HEAD_DIM:base + HEAD_DIM + 1, :]
            out_scr[h * HEAD_DIM:(h + 1) * HEAD_DIM, :] = (
                o * lax.rsqrt(jnp.mean(o * o, axis=0, keepdims=True) + EPS))
        y_ref[...] = (out_scr[...].T * gattn_ref[...]).astype(BF16)

    for tile in range(q_tiles):
        pl.when(pl.program_id(1) == tile)(functools.partial(tile_program, tile))


def _attention_call(q, kaug, vt, kaug_meta, vt_meta, gattn, *, batch, seq, tq):
    cw = q.shape[1]
    tk = vt.shape[2]
    assert tq % tk == 0 and seq % tq == 0
    chunks = seq // tk
    n_heads = cw // HEAD_DIM
    slots = vt.shape[1]
    vt4 = vt.reshape(batch, chunks, slots, tk)
    q_tiles = seq // tq

    def const(shape):
        return pl.BlockSpec(shape, lambda b, i: (0,) * len(shape))

    return pl.pallas_call(
        functools.partial(_attention_body, q_tiles=q_tiles),
        grid=(batch, q_tiles),
        in_specs=[pl.BlockSpec((tq, cw), lambda b, i: (b * q_tiles + i, 0)),
                  pl.BlockSpec((seq, 2 * cw), lambda b, i: (b, 0)),
                  pl.BlockSpec((None, chunks, slots, tk), lambda b, i: (b, 0, 0, 0)),
                  const(kaug_meta.shape), const(vt_meta.shape), const(gattn.shape)],
        out_specs=pl.BlockSpec((tq, cw), lambda b, i: (b * q_tiles + i, 0)),
        out_shape=jax.ShapeDtypeStruct(q.shape, BF16),
        scratch_shapes=[pltpu.VMEM((n_heads, V7X_LANES, tq), BF16),
                        pltpu.VMEM((2, n_heads, tk, tq), F32),
                        pltpu.VMEM((2, n_heads, tq), F32),
                        pltpu.VMEM((n_heads, tq), F32),
                        pltpu.VMEM((slots, tq), F32),
                        pltpu.VMEM((cw, tq), F32)],
        compiler_params=pltpu.CompilerParams(dimension_semantics=("parallel", "parallel"),
                                             vmem_limit_bytes=_vmem_limit()),
        name="fox_attention",
    )(q, kaug, vt4, kaug_meta, vt_meta, gattn)


def _post_mixer_body(h1_ref, yc_ref, ya_ref, wout_ref, g2_ref, wgu_ref, wdown_ref, gf_ref, out_ref, act_scr):
    cw = yc_ref.shape[1]
    parts = _row_parts(h1_ref.shape[0])
    h2 = [h1_ref[r, :]
          + _mxu_dot(yc_ref[r, :], wout_ref[0:cw, :])
          + _mxu_dot(ya_ref[r, :], wout_ref[cw:2 * cw, :]) for r in parts]
    n = [_rms_norm(h, g2_ref[...]).astype(BF16) for h in h2]
    ffn = _swiglu_ffn(n, parts, wgu_ref, wdown_ref, act_scr)
    for h, f, r in zip(h2, ffn, parts):
        out_ref[r, :] = _rms_norm(h + 0.5 * f, gf_ref[...])


def _post_mixer_call(h1, yconv, yattn, wout, g2, wgu, wdown, gf, *, tm):
    n_rows, d = h1.shape
    cw = yconv.shape[1]
    d_ff = wdown.shape[0]

    def resident(shape):
        return pl.BlockSpec(shape, lambda t: (0,) * len(shape), pipeline_mode=pl.Buffered(1))

    def rows(width):
        return pl.BlockSpec((tm, width), lambda t: (t, 0))

    return pl.pallas_call(
        _post_mixer_body,
        grid=(n_rows // tm,),
        in_specs=[rows(d), rows(cw), rows(cw), resident(wout.shape), resident(g2.shape),
                  resident(wgu.shape), resident(wdown.shape), resident(gf.shape)],
        out_specs=rows(d),
        out_shape=jax.ShapeDtypeStruct((n_rows, d), F32),
        scratch_shapes=[pltpu.VMEM((tm, d_ff), BF16)],
        compiler_params=pltpu.CompilerParams(dimension_semantics=("parallel",),
                                             vmem_limit_bytes=_vmem_limit()),
        name="post_mixer",
    )(h1, yconv, yattn, wout, g2, wgu, wdown, gf)


def _vmem_limit():
    return V7X_VMEM_BYTES * 7 // 8


def _layer(x, meta_tokens, ffn1_norm, ffn1_w_gu, ffn1_w_down, mix_norm, w_in, conv_w, b_f, out_norm_conv,
           out_norm_attn, w_out, ffn2_norm, ffn2_w_gu, ffn2_w_down, final_norm, *, tm, tq):
    batch, seq, d = x.shape
    assert ffn1_norm.shape[0] == 1, "one layer deep"
    assert seq % tm == 0 and seq % tq == 0 and tm % ATTN_KEY_CHUNK == 0
    assert conv_w.shape[1] == CONV_K
    cw = conv_w.shape[2]
    n_heads = b_f.shape[1]
    assert w_in.shape[2] == N_PROJ_GROUPS * cw + n_heads and cw == n_heads * HEAD_DIM
    row = lambda a: a.reshape(1, -1).astype(F32)

    wgu1, wdown1 = ffn1_w_gu[0].astype(BF16), ffn1_w_down[0].astype(BF16)
    wgu2, wdown2 = ffn2_w_gu[0].astype(BF16), ffn2_w_down[0].astype(BF16)
    win = w_in[0].astype(BF16)
    wfg = jnp.pad(w_in[0, :, N_PROJ_GROUPS * cw:], ((0, 0), (0, V7X_LANES - n_heads))).astype(BF16)
    bfp = jnp.pad(row(b_f[0]), ((0, 0), (0, V7X_LANES - n_heads)))
    wout = w_out[0].astype(BF16)
    pre_args = (row(ffn1_norm[0]), wgu1, wdown1, row(mix_norm[0]), win, wfg, bfp, conv_w[0].astype(F32),
                row(out_norm_conv[0]))

    meta_rows = V7X_LANES
    x_meta = jnp.pad(meta_tokens.astype(F32), ((0, meta_rows - N_META), (0, 0)))
    _, _, kaug_meta, vt_meta, _, u_meta, g_meta = _pre_mixer_call(
        x_meta, *pre_args, jnp.zeros((V7X_SUBLANES, cw), F32), jnp.zeros((n_heads, V7X_LANES), F32),
        tm=meta_rows, tiles_per_seq=1, tail_start=N_META - V7X_SUBLANES, tk=meta_rows, n_heads=n_heads)

    h1, q, kaug, vt, yconv, _, _ = _pre_mixer_call(
        x.reshape(batch * seq, d), *pre_args, u_meta, g_meta,
        tm=tm, tiles_per_seq=seq // tm, tail_start=tm - V7X_SUBLANES, tk=ATTN_KEY_CHUNK, n_heads=n_heads)

    yattn = _attention_call(q, kaug, vt, kaug_meta, vt_meta.reshape(n_heads * HEAD_SLOT, meta_rows),
                            row(out_norm_attn[0]), batch=batch, seq=seq, tq=tq)

    out = _post_mixer_call(h1, yconv, yattn, wout, row(ffn2_norm[0]), wgu2, wdown2, row(final_norm),
                           tm=2 * tm)
    return out.reshape(batch, seq, d)


def kernel(x, meta_tokens, ffn1_norm, ffn1_w_gu, ffn1_w_down, mix_norm, w_in, conv_w, b_f, out_norm_conv,
           out_norm_attn, w_out, ffn2_norm, ffn2_w_gu, ffn2_w_down, final_norm):
    return _layer(x, meta_tokens, ffn1_norm, ffn1_w_gu, ffn1_w_down, mix_norm, w_in, conv_w, b_f,
                  out_norm_conv, out_norm_attn, w_out, ffn2_norm, ffn2_w_gu, ffn2_w_down, final_norm,
                  tm=TOKEN_TILE, tq=ATTN_TILE)
```

```python
import functools
import math

import jax
import jax.numpy as jnp
from jax import lax
from jax.experimental import pallas as pl
from jax.experimental.pallas import tpu as pltpu

EPS = 1e-6
LOG2_E = math.log2(math.e)
N_META = 16
HEAD_DIM = 64
CONV_K = 3
N_PROJ_GROUPS = 6

V7X_LANES = 128
V7X_SUBLANES = 8
V7X_MXU_DIM = 256
V7X_VMEM_BYTES = 64 * 1024 * 1024

TOKEN_TILE = 512
ROW_SUBTILE = 256
ATTN_KEY_CHUNK = V7X_MXU_DIM
ATTN_TILE = 4 * ATTN_KEY_CHUNK
FF_CHUNK = V7X_MXU_DIM
HEAD_PAIR = 2 * HEAD_DIM
V7X_BF16_ROW_PACK = 2 * V7X_SUBLANES
HEAD_SLOT = HEAD_DIM + V7X_BF16_ROW_PACK

F32 = jnp.float32
BF16 = jnp.bfloat16


def _mxu_dot(a, b):
    return jnp.dot(a, b, preferred_element_type=F32)


def _rms_norm(x, g):
    return x * lax.rsqrt(jnp.mean(x * x, axis=-1, keepdims=True) + EPS) * g


def _row_parts(tm):
    sub = min(tm, ROW_SUBTILE)
    return [slice(r, r + sub) for r in range(0, tm, sub)]


def _swiglu_ffn(ns, parts, wgu_ref, wdown_ref, act_scr):
    d_ff = wdown_ref.shape[0]
    for n, r in zip(ns, parts):
        for lo in range(0, d_ff, FF_CHUNK):
            g = _mxu_dot(n, wgu_ref[:, lo:lo + FF_CHUNK])
            u = _mxu_dot(n, wgu_ref[:, d_ff + lo:d_ff + lo + FF_CHUNK])
            act_scr[r, lo:lo + FF_CHUNK] = (g * jax.nn.sigmoid(g) * u).astype(BF16)
    return [_mxu_dot(act_scr[r, :], wdown_ref[...]) for r in parts]


def _group_norm_lanes(y, g):
    lane = lax.broadcasted_iota(jnp.int32, (y.shape[0], V7X_LANES), 1)
    low = lane < HEAD_DIM
    outs = []
    for c in range(0, y.shape[1], V7X_LANES):
        blk = y[:, c:c + V7X_LANES]
        sq = blk * blk
        ms_lo = jnp.sum(jnp.where(low, sq, 0.0), axis=-1, keepdims=True) * (1.0 / HEAD_DIM)
        ms_hi = jnp.sum(jnp.where(low, 0.0, sq), axis=-1, keepdims=True) * (1.0 / HEAD_DIM)
        inv = jnp.where(low, lax.rsqrt(ms_lo + EPS), lax.rsqrt(ms_hi + EPS))
        outs.append(blk * inv * g[:, c:c + V7X_LANES])
    return jnp.concatenate(outs, axis=-1)


def _log_sigmoid(x):
    z = -x
    return -(jnp.maximum(z, 0.0) + jnp.log1p(jnp.exp(-jnp.abs(z))))


def _cumsum_lanes(a):
    r, n = a.shape
    lane = lax.broadcasted_iota(jnp.int32, (r, V7X_LANES), 1)
    blocks = []
    offset = jnp.zeros((r, 1), F32)
    for c in range(0, n, V7X_LANES):
        blk = a[:, c:c + V7X_LANES]
        shift = 1
        while shift < V7X_LANES:
            blk = blk + jnp.where(lane >= shift, pltpu.roll(blk, shift, axis=1), 0.0)
            shift *= 2
        blk = blk + offset
        offset = blk[:, V7X_LANES - 1:V7X_LANES]
        blocks.append(blk)
    return jnp.concatenate(blocks, axis=-1)


GATE_PIECES = 3


def _gate_pieces(g, n_heads):
    lane = lax.broadcasted_iota(jnp.int32, g.shape, 1)
    hi = g.astype(BF16).astype(F32)
    rest = g - hi
    mid = rest.astype(BF16).astype(F32)
    pieces = (hi, mid, rest - mid)
    out = jnp.zeros_like(g)
    for j, piece in enumerate(pieces):
        lo = HEAD_DIM + j * n_heads
        out = jnp.where((lane >= lo) & (lane < lo + n_heads), pltpu.roll(piece, lo, axis=1), out)
    return out


def _key_block(k_pair, half, gate):
    lane = lax.broadcasted_iota(jnp.int32, k_pair.shape, 1)
    blk = pltpu.roll(k_pair, HEAD_DIM, axis=1) if half else k_pair
    return jnp.where(lane < HEAD_DIM, blk, gate).astype(BF16)


def _pre_mixer_body(x_ref, g1_ref, wgu_ref, wdown_ref, gm_ref, win_ref, wfg_ref, bf_ref, convw_ref,
                    gconv_ref, uinit_ref, ginit_ref,
                    h1_ref, q_ref, kaug_ref, vt_ref, yconv_ref, utail_ref, gtail_ref,
                    act_scr, u_scr, gcarry_scr, *, tiles_per_seq, tail_start, n_heads):
    tm = x_ref.shape[0]
    cw = q_ref.shape[1]
    halo = V7X_SUBLANES

    @pl.when(pl.program_id(0) % tiles_per_seq == 0)
    def _():
        u_scr[0:halo, :] = uinit_ref[...]
        gcarry_scr[...] = ginit_ref[...]

    parts = _row_parts(tm)
    sub = parts[0].stop
    xs = [x_ref[r, :] for r in parts]
    n1 = [_rms_norm(x, g1_ref[...]).astype(BF16) for x in xs]
    ffn = _swiglu_ffn(n1, parts, wgu_ref, wdown_ref, act_scr)
    n2 = []
    for x, f, r in zip(xs, ffn, parts):
        h1 = x + 0.5 * f
        h1_ref[r, :] = h1
        n2.append(_rms_norm(h1, gm_ref[...]).astype(BF16))

    def proj(n, j):
        return _mxu_dot(n, win_ref[:, j * cw:(j + 1) * cw])

    tail_col = tail_start + halo - 1
    g_carry = gcarry_scr[0:n_heads, 0:1]
    gates = []
    for n, r in zip(n2, parts):
        fg = _mxu_dot(n, wfg_ref[...]) + bf_ref[...]
        g_run = _cumsum_lanes(_log_sigmoid(fg).T[0:n_heads, :]) + g_carry
        g_carry = g_run[:, sub - 1:sub]
        if r.start <= tail_col < r.stop:
            gtail_ref[...] = jnp.broadcast_to(g_run[:, tail_col - r.start:tail_col - r.start + 1],
                                              gtail_ref.shape)
        g_rows = jnp.concatenate([g_run * LOG2_E, jnp.zeros((V7X_LANES - n_heads, sub), F32)], axis=0)
        gates.append(_gate_pieces(g_rows.T, n_heads))
    gcarry_scr[...] = jnp.broadcast_to(g_carry, gcarry_scr.shape)

    us = [proj(n, 1) * proj(n, 2) for n in n2]
    for u, r in zip(us, parts):
        u_scr[halo + r.start:halo + r.stop, :] = u
    for n, u, r in zip(n2, us, parts):
        conv = (convw_ref[0:1, :] * u_scr[pl.ds(halo - 2 + r.start, sub), :]
                + convw_ref[1:2, :] * u_scr[pl.ds(halo - 1 + r.start, sub), :]
                + convw_ref[2:3, :] * u)
        yconv_ref[r, :] = _group_norm_lanes(proj(n, 0) * conv, gconv_ref[...]).astype(BF16)
    utail_ref[...] = u_scr[halo + tail_start:halo + tail_start + halo, :]
    u_scr[0:halo, :] = u_scr[tm:tm + halo, :]

    for n, gate, r in zip(n2, gates, parts):
        k = proj(n, 4)
        for h in range(n_heads):
            pair, half = divmod(h, 2)
            kaug_ref[r, h * V7X_LANES:(h + 1) * V7X_LANES] = _key_block(
                k[:, pair * HEAD_PAIR:(pair + 1) * HEAD_PAIR], half, gate)
    tk = vt_ref.shape[2]
    pad_rows = HEAD_SLOT - HEAD_DIM
    ones_row = jnp.where(lax.broadcasted_iota(jnp.int32, (pad_rows, tk), 0) == 0, 1.0, 0.0).astype(BF16)
    for n, r in zip(n2, parts):
        vt = proj(n, 5).T
        for c in range(sub // tk):
            chunk = r.start // tk + c
            for h in range(n_heads):
                vt_ref[chunk, h * HEAD_SLOT:h * HEAD_SLOT + HEAD_DIM, :] = (
                    vt[h * HEAD_DIM:(h + 1) * HEAD_DIM, c * tk:(c + 1) * tk].astype(BF16))
                vt_ref[chunk, h * HEAD_SLOT + HEAD_DIM:(h + 1) * HEAD_SLOT, :] = ones_row
    for n, r in zip(n2, parts):
        q_ref[r, :] = (proj(n, 3) * (HEAD_DIM ** -0.5 * LOG2_E)).astype(BF16)


def _pre_mixer_call(x2, g1, wgu, wdown, gm, win, wfg, bfp, convw, gconv, uinit, ginit, *, tm, tiles_per_seq,
                    tail_start, tk, n_heads):
    n_rows, d = x2.shape
    cw = gconv.shape[1]
    d_ff = wdown.shape[0]
    n_tiles = n_rows // tm
    halo = V7X_SUBLANES

    def resident(shape):
        return pl.BlockSpec(shape, lambda t: (0,) * len(shape), pipeline_mode=pl.Buffered(1))

    def rows(width):
        return pl.BlockSpec((tm, width), lambda t: (t, 0))

    body = functools.partial(_pre_mixer_body, tiles_per_seq=tiles_per_seq, tail_start=tail_start,
                             n_heads=n_heads)
    return pl.pallas_call(
        body,
        grid=(n_tiles,),
        in_specs=[rows(d), resident(g1.shape), resident(wgu.shape), resident(wdown.shape),
                  resident(gm.shape), resident((d, N_PROJ_GROUPS * cw)), resident(wfg.shape),
                  resident(bfp.shape),
                  resident(convw.shape), resident(gconv.shape), resident(uinit.shape),
                  resident(ginit.shape)],
        out_specs=[rows(d), rows(cw), rows(2 * cw),
                   pl.BlockSpec((tm // tk, n_heads * HEAD_SLOT, tk), lambda t: (t, 0, 0)),
                   rows(cw),
                   pl.BlockSpec((halo, cw), lambda t: (t, 0)),
                   pl.BlockSpec((n_heads, V7X_LANES), lambda t: (t, 0))],
        out_shape=[jax.ShapeDtypeStruct((n_rows, d), F32),
                   jax.ShapeDtypeStruct((n_rows, cw), BF16),
                   jax.ShapeDtypeStruct((n_rows, 2 * cw), BF16),
                   jax.ShapeDtypeStruct((n_rows // tk, n_heads * HEAD_SLOT, tk), BF16),
                   jax.ShapeDtypeStruct((n_rows, cw), BF16),
                   jax.ShapeDtypeStruct((n_tiles * halo, cw), F32),
                   jax.ShapeDtypeStruct((n_tiles * n_heads, V7X_LANES), F32)],
        scratch_shapes=[pltpu.VMEM((tm, d_ff), BF16),
                        pltpu.VMEM((tm + halo, cw), F32),
                        pltpu.VMEM((n_heads, V7X_LANES), F32)],
        compiler_params=pltpu.CompilerParams(dimension_semantics=("arbitrary",),
                                             vmem_limit_bytes=_vmem_limit()),
        name="pre_mixer",
    )(x2, g1, wgu, wdown, gm, win, wfg, bfp, convw, gconv, uinit, ginit)


def _attention_body(q_ref, kaug_ref, vt_ref, kmeta_ref, vtmeta_ref, gattn_ref, y_ref,
                    qm_scr, s_scr, cmax_scr, m_scr, acc_scr, out_scr, *, q_tiles):
    tq = q_ref.shape[0]
    tk = vt_ref.shape[2]
    n_heads = q_ref.shape[1] // HEAD_DIM
    neg_inf = float("-inf")
    piece_row = lax.broadcasted_iota(jnp.int32, (HEAD_DIM, tq), 0)

    def build_query_operands(pair):
        qt = q_ref[:, pair * HEAD_PAIR:(pair + 1) * HEAD_PAIR].astype(F32).T.astype(BF16)
        for half in range(2):
            h = 2 * pair + half
            mine = piece_row == h
            for j in range(1, GATE_PIECES):
                mine = mine | (piece_row == j * n_heads + h)
            qm_scr[h, 0:HEAD_DIM, :] = qt[half * HEAD_DIM:(half + 1) * HEAD_DIM, :]
            qm_scr[h, HEAD_DIM:2 * HEAD_DIM, :] = jnp.where(mine, -1.0, 0.0).astype(BF16)

    all_q = slice(0, tq)

    def qk(kaug_rows, h, lanes):
        return _mxu_dot(kaug_rows[:, h * V7X_LANES:(h + 1) * V7X_LANES], qm_scr[h, :, lanes])

    def key_rows(c):
        return kaug_ref.at[c * tk:(c + 1) * tk, :]

    def masked_scores(kaug_rows, h, lanes, mask):
        s = qk(kaug_rows, h, lanes)
        if mask is not None:
            n = lanes.stop - lanes.start
            masked = jnp.where(mask, s[:, 0:mask.shape[1]], neg_inf)
            s = masked if mask.shape[1] == n else jnp.concatenate([masked, s[:, mask.shape[1]:]], axis=1)
        return s, jnp.max(s, axis=0, keepdims=True)


    def fold_head(h, s, c_max, vt_blk, first, lanes):
        n = lanes.stop - lanes.start
        slot_rows = slice(h * HEAD_SLOT, (h + 1) * HEAD_SLOT)
        if first:
            m_new = c_max
        else:
            m_old = m_scr[h:h + 1, lanes]
            m_new = jnp.maximum(m_old, c_max)
            alpha = jnp.exp2(m_old - m_new)
        p = jnp.exp2(s - m_new).astype(BF16)
        if vt_blk.shape[1] > p.shape[0]:
            p = jnp.concatenate([p, jnp.zeros((vt_blk.shape[1] - p.shape[0], n), BF16)], axis=0)
        m_scr[h:h + 1, lanes] = m_new
        pv = _mxu_dot(vt_blk[slot_rows, 0:V7X_LANES], p[0:V7X_LANES, :])
        for k0 in range(V7X_LANES, p.shape[0], V7X_LANES):
            pv = pv + _mxu_dot(vt_blk[slot_rows, k0:k0 + V7X_LANES], p[k0:k0 + V7X_LANES, :])
        acc_scr[slot_rows, lanes] = pv if first else alpha * acc_scr[slot_rows, lanes] + pv

    def tile_program(i):
        causal = (lax.broadcasted_iota(jnp.int32, (tk, tk), 0)
                  <= lax.broadcasted_iota(jnp.int32, (tk, tk), 1))
        ratio = tq // tk
        first_diag = ratio * i
        n_chunks = first_diag + ratio

        def lanes_of(c):
            return all_q if c < first_diag else slice((c - first_diag) * tk, tq)

        def stage_head(c, h):
            lanes = lanes_of(c)
            s, c_max = masked_scores(key_rows(c), h, lanes, causal if c >= first_diag else None)
            s_scr[c % 2, h, :, lanes] = s
            cmax_scr[c % 2, h:h + 1, lanes] = c_max

        def fold_staged(c, h):
            lanes = lanes_of(c)
            fold_head(h, s_scr[c % 2, h, :, lanes], cmax_scr[c % 2, h:h + 1, lanes], vt_ref.at[c], False, lanes)

        meta_rows = kmeta_ref.at[0:N_META, :]
        meta = []
        for pair in range(n_heads // 2):
            build_query_operands(pair)
            for h in (2 * pair, 2 * pair + 1):
                meta.append(masked_scores(meta_rows, h, all_q, None))
                stage_head(0, h)
        for h in range(n_heads):
            fold_head(h, meta[h][0], meta[h][1], vtmeta_ref, True, all_q)

        for c in range(n_chunks):
            for h in range(n_heads):
                if c + 1 < n_chunks:
                    stage_head(c + 1, h)
                fold_staged(c, h)

        for h in range(n_heads):
            base = h * HEAD_SLOT
            o = acc_scr[base:base + HEAD_DIM, :] / acc_scr[base + HEAD_DIM:base + HEAD_DIM + 1, :]
            out_scr[h * HEAD_DIM:(h + 1) * HEAD_DIM, :] = (
                o * lax.rsqrt(jnp.mean(o * o, axis=0, keepdims=True) + EPS))
        y_ref[...] = (out_scr[...].T * gattn_ref[...]).astype(BF16)

    for tile in range(q_tiles):
        pl.when(pl.program_id(1) == tile)(functools.partial(tile_program, tile))


def _attention_call(q, kaug, vt, kaug_meta, vt_meta, gattn, *, batch, seq, tq):
    cw = q.shape[1]
    tk = vt.shape[2]
    assert tq % tk == 0 and seq % tq == 0
    chunks = seq // tk
    n_heads = cw // HEAD_DIM
    slots = vt.shape[1]
    vt4 = vt.reshape(batch, chunks, slots, tk)
    q_tiles = seq // tq

    def const(shape):
        return pl.BlockSpec(shape, lambda b, i: (0,) * len(shape))

    return pl.pallas_call(
        functools.partial(_attention_body, q_tiles=q_tiles),
        grid=(batch, q_tiles),
        in_specs=[pl.BlockSpec((tq, cw), lambda b, i: (b * q_tiles + i, 0)),
                  pl.BlockSpec((seq, 2 * cw), lambda b, i: (b, 0)),
                  pl.BlockSpec((None, chunks, slots, tk), lambda b, i: (b, 0, 0, 0)),
                  const(kaug_meta.shape), const(vt_meta.shape), const(gattn.shape)],
        out_specs=pl.BlockSpec((tq, cw), lambda b, i: (b * q_tiles + i, 0)),
        out_shape=jax.ShapeDtypeStruct(q.shape, BF16),
        scratch_shapes=[pltpu.VMEM((n_heads, V7X_LANES, tq), BF16),
                        pltpu.VMEM((2, n_heads, tk, tq), F32),
                        pltpu.VMEM((2, n_heads, tq), F32),
                        pltpu.VMEM((n_heads, tq), F32),
                        pltpu.VMEM((slots, tq), F32),
                        pltpu.VMEM((cw, tq), F32)],
        compiler_params=pltpu.CompilerParams(dimension_semantics=("parallel", "parallel"),
                                             vmem_limit_bytes=_vmem_limit()),
        name="fox_attention",
    )(q, kaug, vt4, kaug_meta, vt_meta, gattn)


def _post_mixer_body(h1_ref, yc_ref, ya_ref, wout_ref, g2_ref, wgu_ref, wdown_ref, gf_ref, out_ref, act_scr):
    cw = yc_ref.shape[1]
    parts = _row_parts(h1_ref.shape[0])
    h2 = [h1_ref[r, :]
          + _mxu_dot(yc_ref[r, :], wout_ref[0:cw, :])
          + _mxu_dot(ya_ref[r, :], wout_ref[cw:2 * cw, :]) for r in parts]
    n = [_rms_norm(h, g2_ref[...]).astype(BF16) for h in h2]
    ffn = _swiglu_ffn(n, parts, wgu_ref, wdown_ref, act_scr)
    for h, f, r in zip(h2, ffn, parts):
        out_ref[r, :] = _rms_norm(h + 0.5 * f, gf_ref[...])


def _post_mixer_call(h1, yconv, yattn, wout, g2, wgu, wdown, gf, *, tm):
    n_rows, d = h1.shape
    cw = yconv.shape[1]
    d_ff = wdown.shape[0]

    def resident(shape):
        return pl.BlockSpec(shape, lambda t: (0,) * len(shape), pipeline_mode=pl.Buffered(1))

    def rows(width):
        return pl.BlockSpec((tm, width), lambda t: (t, 0))

    return pl.pallas_call(
        _post_mixer_body,
        grid=(n_rows // tm,),
        in_specs=[rows(d), rows(cw), rows(cw), resident(wout.shape), resident(g2.shape),
                  resident(wgu.shape), resident(wdown.shape), resident(gf.shape)],
        out_specs=rows(d),
        out_shape=jax.ShapeDtypeStruct((n_rows, d), F32),
        scratch_shapes=[pltpu.VMEM((tm, d_ff), BF16)],
        compiler_params=pltpu.CompilerParams(dimension_semantics=("parallel",),
                                             vmem_limit_bytes=_vmem_limit()),
        name="post_mixer",
    )(h1, yconv, yattn, wout, g2, wgu, wdown, gf)


def _vmem_limit():
    return V7X_VMEM_BYTES * 7 // 8


def _layer(x, meta_tokens, ffn1_norm, ffn1_w_gu, ffn1_w_down, mix_norm, w_in, conv_w, b_f, out_norm_conv,
           out_norm_attn, w_out, ffn2_norm, ffn2_w_gu, ffn2_w_down, final_norm, *, tm, tq):
    batch, seq, d = x.shape
    assert ffn1_norm.shape[0] == 1, "one layer deep"
    assert seq % tm == 0 and seq % tq == 0 and tm % ATTN_KEY_CHUNK == 0
    assert conv_w.shape[1] == CONV_K
    cw = conv_w.shape[2]
    n_heads = b_f.shape[1]
    assert w_in.shape[2] == N_PROJ_GROUPS * cw + n_heads and cw == n_heads * HEAD_DIM
    row = lambda a: a.reshape(1, -1).astype(F32)

    wgu1, wdown1 = ffn1_w_gu[0].astype(BF16), ffn1_w_down[0].astype(BF16)
    wgu2, wdown2 = ffn2_w_gu[0].astype(BF16), ffn2_w_down[0].astype(BF16)
    win = w_in[0].astype(BF16)
    wfg = jnp.pad(w_in[0, :, N_PROJ_GROUPS * cw:], ((0, 0), (0, V7X_LANES - n_heads))).astype(BF16)
    bfp = jnp.pad(row(b_f[0]), ((0, 0), (0, V7X_LANES - n_heads)))
    wout = w_out[0].astype(BF16)
    pre_args = (row(ffn1_norm[0]), wgu1, wdown1, row(mix_norm[0]), win, wfg, bfp, conv_w[0].astype(F32),
                row(out_norm_conv[0]))

    meta_rows = V7X_LANES
    x_meta = jnp.pad(meta_tokens.astype(F32), ((0, meta_rows - N_META), (0, 0)))
    _, _, kaug_meta, vt_meta, _, u_meta, g_meta = _pre_mixer_call(
        x_meta, *pre_args, jnp.zeros((V7X_SUBLANES, cw), F32), jnp.zeros((n_heads, V7X_LANES), F32),
        tm=meta_rows, tiles_per_seq=1, tail_start=N_META - V7X_SUBLANES, tk=meta_rows, n_heads=n_heads)

    h1, q, kaug, vt, yconv, _, _ = _pre_mixer_call(
        x.reshape(batch * seq, d), *pre_args, u_meta, g_meta,
        tm=tm, tiles_per_seq=seq // tm, tail_start=tm - V7X_SUBLANES, tk=ATTN_KEY_CHUNK, n_heads=n_heads)

    yattn = _attention_call(q, kaug, vt, kaug_meta, vt_meta.reshape(n_heads * HEAD_SLOT, meta_rows),
                            row(out_norm_attn[0]), batch=batch, seq=seq, tq=tq)

    out = _post_mixer_call(h1, yconv, yattn, wout, row(ffn2_norm[0]), wgu2, wdown2, row(final_norm),
                           tm=2 * tm)
    return out.reshape(batch, seq, d)


def kernel(x, meta_tokens, ffn1_norm, ffn1_w_gu, ffn1_w_down, mix_norm, w_in, conv_w, b_f, out_norm_conv,
           out_norm_attn, w_out, ffn2_norm, ffn2_w_gu, ffn2_w_down, final_norm):
    return _layer(x, meta_tokens, ffn1_norm, ffn1_w_gu, ffn1_w_down, mix_norm, w_in, conv_w, b_f,
                  out_norm_conv, out_norm_attn, w_out, ffn2_norm, ffn2_w_gu, ffn2_w_down, final_norm,
                  tm=TOKEN_TILE, tq=ATTN_TILE)
```
